```python
import math
import jax, jax.numpy as jnp
from jax import lax
import numpy as np

D_MODEL = 1024
BATCH = 4
SEQ = 8192
DEPTH = 1

ATT_HEADS = 8
ATT_QK_DIM = 64
ATT_V_DIM = 2 * ATT_QK_DIM
ATT_QK_WIDTH = ATT_HEADS * 2 * ATT_QK_DIM
ATT_V_WIDTH = ATT_HEADS * ATT_V_DIM
REC_HEADS = 8
REC_K_DIM = 128
REC_V_DIM = 128
REC_F_WIDTH = REC_HEADS * REC_K_DIM
REC_V_WIDTH = REC_HEADS * REC_V_DIM
REC_CHUNK = 64
D_FF = 4 * D_MODEL
ROPE_THETA = 500000.0
ROPE_DIM = ATT_QK_DIM // 4
Q_BLOCK = 128
EPS = 1e-6
N_ADA = 6
IN_WIDTHS = (ATT_QK_WIDTH, ATT_QK_WIDTH, ATT_V_WIDTH,
             REC_F_WIDTH, REC_F_WIDTH, REC_V_WIDTH, REC_V_WIDTH,
             D_MODEL, D_MODEL)
IN_WIDTH = sum(IN_WIDTHS)

kernel_name = "hybrid_diffattn_hgrn2_gated_merge"


def rmsnorm(x, w):
    xf = x.astype(jnp.float32)
    y = xf * lax.rsqrt(jnp.mean(xf * xf, axis=-1, keepdims=True) + EPS)
    return (y * w.astype(jnp.float32)).astype(x.dtype)


def lambda_init_fn(layer_idx):
    return 0.8 - 0.6 * math.exp(-0.3 * layer_idx)


def rope_partial(t, cos, sin):
    half = ROPE_DIM // 2
    t1, t2, rest = t[..., :half], t[..., half:ROPE_DIM], t[..., ROPE_DIM:]
    cos = cos.astype(t.dtype)
    sin = sin.astype(t.dtype)
    return jnp.concatenate([t1 * cos - t2 * sin, t2 * cos + t1 * sin, rest], axis=-1)


def diff_attention(q, k, v, lam):
    B, S = q.shape[0], q.shape[1]
    nb = S // Q_BLOCK
    qb = q.reshape(B, nb, Q_BLOCK, ATT_HEADS, 2, ATT_QK_DIM).transpose(1, 0, 2, 3, 4, 5)
    key_idx = jnp.arange(S)
    scale = ATT_QK_DIM ** -0.5

    def block(args):
        qi, i = args
        s = jnp.einsum('bqhmd,bkhmd->bhmqk', qi, k).astype(jnp.float32) * scale
        q_idx = i * Q_BLOCK + jnp.arange(Q_BLOCK)
        mask = key_idx[None, :] <= q_idx[:, None]
        p = jax.nn.softmax(jnp.where(mask, s, -jnp.inf), axis=-1)
        a = (p[:, :, 0] - lam * p[:, :, 1]).astype(v.dtype)
        return jnp.einsum('bhqk,bkhe->bqhe', a, v)

    o = lax.map(block, (qb, jnp.arange(nb)))
    return o.transpose(1, 0, 2, 3, 4).reshape(B, S, ATT_HEADS, ATT_V_DIM)


def hgrn2_chunkwise(q, k, v, log_f):
    B, S = q.shape[0], q.shape[1]
    nc = S // REC_CHUNK

    def to_chunks(t):
        return t.astype(jnp.float32).reshape(B, nc, REC_CHUNK, REC_HEADS, t.shape[-1]).transpose(1, 0, 3, 2, 4)

    causal = jnp.tril(jnp.ones((REC_CHUNK, REC_CHUNK), dtype=bool))

    def step(state, inp):
        qc, kc, vc, gc = inp
        b = jnp.cumsum(gc, axis=2)
        rel = jnp.where(causal[:, :, None], b[:, :, :, None, :] - b[:, :, None, :, :], -jnp.inf)
        scores = jnp.einsum('bhtk,bhsk,bhtsk->bhts', qc, kc, jnp.exp(rel))
        o = (jnp.einsum('bhts,bhsv->bhtv', scores, vc)
             + jnp.einsum('bhtk,bhkv->bhtv', qc * jnp.exp(b), state))
        b_last = b[:, :, -1:, :]
        state = (jnp.exp(b_last[:, :, 0, :, None]) * state
                 + jnp.einsum('bhsk,bhsv->bhkv', kc * jnp.exp(b_last - b), vc))
        return state, o

    s0 = jnp.zeros((B, REC_HEADS, REC_K_DIM, REC_V_DIM), jnp.float32)
    _, o = lax.scan(step, s0, (to_chunks(q), to_chunks(k), to_chunks(v), to_chunks(log_f)))
    return o.transpose(1, 0, 3, 2, 4).reshape(B, S, REC_HEADS, REC_V_DIM)


def setup_inputs(seed: int = 0) -> dict:
    key = jax.random.key(seed)
    ks = jax.random.split(key, 24)
    nrm = jax.random.normal
    f32 = jnp.float32

    def gain(k, shape):
        return 1.0 + 0.02 * nrm(k, shape, f32)

    offsets = jax.random.randint(ks[2], (BATCH, 1), 0, 4096, dtype=jnp.int32)
    positions = offsets + jnp.arange(SEQ, dtype=jnp.int32)[None, :]
    return {
        "x": nrm(ks[0], (BATCH, SEQ, D_MODEL), f32),
        "c": nrm(ks[1], (BATCH, D_MODEL), f32),
        "positions": positions,
        "w_ada": nrm(ks[3], (DEPTH, D_MODEL, N_ADA * D_MODEL), f32) * D_MODEL ** -0.5,
        "b_ada": 0.02 * nrm(ks[4], (DEPTH, N_ADA * D_MODEL), f32),
        "norm_mix": gain(ks[5], (DEPTH, D_MODEL)),
        "w_in": nrm(ks[6], (DEPTH, D_MODEL, IN_WIDTH), f32) * D_MODEL ** -0.5,
        "lam_q1": 0.1 * nrm(ks[7], (DEPTH, ATT_QK_DIM), f32),
        "lam_k1": 0.1 * nrm(ks[8], (DEPTH, ATT_QK_DIM), f32),
        "lam_q2": 0.1 * nrm(ks[9], (DEPTH, ATT_QK_DIM), f32),
        "lam_k2": 0.1 * nrm(ks[10], (DEPTH, ATT_QK_DIM), f32),
        "subln_w": gain(ks[11], (DEPTH, ATT_V_DIM)),
        "lb_logits": 0.5 * nrm(ks[12], (DEPTH + 1, REC_F_WIDTH), f32),
        "rec_norm_w": gain(ks[13], (DEPTH, REC_V_DIM)),
        "w_proj_att": nrm(ks[14], (DEPTH, ATT_V_WIDTH, D_MODEL), f32) * ATT_V_WIDTH ** -0.5,
        "w_proj_rec": nrm(ks[15], (DEPTH, REC_V_WIDTH, D_MODEL), f32) * REC_V_WIDTH ** -0.5,
        "w_out": nrm(ks[16], (DEPTH, D_MODEL, D_MODEL), f32) * D_MODEL ** -0.5,
        "norm_mlp": gain(ks[17], (DEPTH, D_MODEL)),
        "w_mlp_in": nrm(ks[18], (DEPTH, D_MODEL, D_FF), f32) * D_MODEL ** -0.5,
        "w_mlp_out": nrm(ks[19], (DEPTH, D_FF, D_MODEL), f32) * D_FF ** -0.5,
        "norm_final": gain(ks[20], (D_MODEL,)),
    }


def reference(x, c, positions, w_ada, b_ada, norm_mix, w_in, lam_q1, lam_k1, lam_q2, lam_k2,
              subln_w, lb_logits, rec_norm_w, w_proj_att, w_proj_rec, w_out, norm_mlp,
              w_mlp_in, w_mlp_out, norm_final):
    B, S, _ = x.shape
    f32 = jnp.float32
    split_points = np.cumsum(IN_WIDTHS)[:-1].tolist()

    inv_freq = ROPE_THETA ** (-jnp.arange(0, ROPE_DIM, 2, dtype=f32) / ROPE_DIM)
    ang = positions.astype(f32)[..., None] * inv_freq
    cos = jnp.cos(ang)[:, :, None, None, :]
    sin = jnp.sin(ang)[:, :, None, None, :]

    lb_p = jax.nn.softmax(lb_logits.astype(f32), axis=0)
    lb_cum = jnp.cumsum(lb_p, axis=0) - lb_p[0]

    cond = jax.nn.silu(c)
    for l in range(DEPTH):
        ada = cond @ w_ada[l] + b_ada[l]
        sh_m, sc_m, g_m, sh_f, sc_f, g_f = [a[:, None, :] for a in jnp.split(ada, N_ADA, axis=-1)]

        h = rmsnorm(x, norm_mix[l]) * (1.0 + sc_m) + sh_m
        proj = h @ w_in[l]
        q, k, v, rq, rf, ri, rg, ga, gr = jnp.split(proj, split_points, axis=-1)

        lam_init = lambda_init_fn(l)
        lam = (jnp.exp(jnp.sum(lam_q1[l].astype(f32) * lam_k1[l].astype(f32)))
               - jnp.exp(jnp.sum(lam_q2[l].astype(f32) * lam_k2[l].astype(f32))) + lam_init)
        qa = rope_partial(q.reshape(B, S, ATT_HEADS, 2, ATT_QK_DIM), cos, sin)
        ka = rope_partial(k.reshape(B, S, ATT_HEADS, 2, ATT_QK_DIM), cos, sin)
        va = v.reshape(B, S, ATT_HEADS, ATT_V_DIM)
        o_a = diff_attention(qa, ka, va, lam)
        o_a = (rmsnorm(o_a, subln_w[l]) * (1.0 - lam_init)).reshape(B, S, ATT_V_WIDTH)

        lb = lb_cum[l + 1]
        f = lb + (1.0 - lb) * jax.nn.sigmoid(rf.astype(f32))
        heads_k = lambda t: t.reshape(B, S, REC_HEADS, REC_K_DIM)
        o_r = hgrn2_chunkwise(heads_k(rq), heads_k(1.0 - f), ri.reshape(B, S, REC_HEADS, REC_V_DIM),
                              heads_k(jnp.log(f)))
        o_r = o_r.astype(x.dtype)
        o_r = (rmsnorm(o_r, rec_norm_w[l]) * jax.nn.silu(rg.reshape(B, S, REC_HEADS, REC_V_DIM))
               ).reshape(B, S, REC_V_WIDTH)

        y = jax.nn.sigmoid(ga) * (o_a @ w_proj_att[l]) + jax.nn.sigmoid(gr) * (o_r @ w_proj_rec[l])
        x = x + g_m * (y @ w_out[l])

        h = rmsnorm(x, norm_mlp[l]) * (1.0 + sc_f) + sh_f
        u = jnp.square(jax.nn.relu(h @ w_mlp_in[l]))
        x = x + g_f * (u @ w_mlp_out[l])

    return rmsnorm(x, norm_final)
```

```python
import functools
import math

import jax
import jax.numpy as jnp
from jax import lax
from jax.experimental import pallas as pl
from jax.experimental.pallas import tpu as pltpu

F32 = jnp.float32
BF16 = jnp.bfloat16

D_MODEL = 1024
HEADS = 8
HEAD_W = 128
QK_DIM = 64
ROPE_DIM = 16
ROPE_THETA = 500000.0
D_FF = 4 * D_MODEL
N_ADA = 6
N_SECTIONS = 9
REC_CHUNK = 64
EPS = 1e-6
LAM_INIT = 0.8 - 0.6 * math.exp(-0.3 * 0)
SAFE_DECAY = 80.0
VMEM_LIMIT = 56 * 1024 * 1024

_NT = (((1,), (1,)), ((), ()))
_TN = (((0,), (0,)), ((), ()))


def _rms(x):
    return x * lax.rsqrt(jnp.mean(x * x, axis=-1, keepdims=True) + EPS)


def _ada_kernel(c_ref, w_ref, b_ref, o_ref):
    c = c_ref[...]
    cond = c * jax.nn.sigmoid(c)
    o_ref[...] = jnp.dot(cond, w_ref[...], preferred_element_type=F32,
                         precision=lax.Precision.HIGHEST) + b_ref[...]


def _ada(c_pad, w_ada, b_ada):
    n = w_ada.shape[1]
    return pl.pallas_call(
        _ada_kernel,
        grid=(n // D_MODEL,),
        in_specs=[pl.BlockSpec((8, D_MODEL), lambda j: (0, 0)),
                  pl.BlockSpec((D_MODEL, D_MODEL), lambda j: (0, j)),
                  pl.BlockSpec((1, D_MODEL), lambda j: (0, j))],
        out_specs=pl.BlockSpec((8, D_MODEL), lambda j: (0, j)),
        out_shape=jax.ShapeDtypeStruct((8, n), F32),
        name="ada",
    )(c_pad, w_ada, b_ada)


def _inproj_kernel(x_ref, ada_ref, nw_ref, w_ref, pos_ref, invf_ref, lb_ref,
                   q_ref, k_ref, v_ref, rq_ref, g_ref, ri_ref, rg_ref, ga_ref, gr_ref,
                   h_scr, cos_scr, s1_scr, s2_scr):
    j = pl.program_id(1)

    @pl.when(j == 0)
    def _():
        ada = ada_ref[0]
        h = _rms(x_ref[...]) * nw_ref[...] * (1.0 + ada[1:2]) + ada[0:1]
        h_scr[...] = h.astype(BF16)
        ang = pos_ref[...].astype(F32) * invf_ref[...]
        lane = lax.broadcasted_iota(jnp.int32, (1, HEAD_W), 1) % QK_DIM
        c = jnp.cos(ang)
        s = jnp.sin(ang)
        cos_scr[...] = jnp.where(lane < ROPE_DIM, c, 1.0)
        s1_scr[...] = jnp.where(lane < ROPE_DIM // 2, -s, 0.0)
        s2_scr[...] = jnp.where((lane >= ROPE_DIM // 2) & (lane < ROPE_DIM), s, 0.0)

    acc = jnp.dot(h_scr[...], w_ref[...], preferred_element_type=F32)

    def rope_store(ref, scale):
        cs, s1, s2 = cos_scr[...], s1_scr[...], s2_scr[...]
        for hh in range(HEADS):
            t = acc[:, hh * HEAD_W:(hh + 1) * HEAD_W]
            r = t * cs + pltpu.roll(t, HEAD_W - ROPE_DIM // 2, 1) * s1 + pltpu.roll(t, ROPE_DIM // 2, 1) * s2
            if scale != 1.0:
                r = r * scale
            ref[:, hh * HEAD_W:(hh + 1) * HEAD_W] = r.astype(ref.dtype)

    @pl.when(j == 0)
    def _():
        rope_store(q_ref, QK_DIM ** -0.5)

    @pl.when(j == 1)
    def _():
        rope_store(k_ref, 1.0)

    @pl.when(j == 2)
    def _():
        v_ref[...] = acc.astype(v_ref.dtype)

    @pl.when(j == 3)
    def _():
        rq_ref[...] = acc.astype(rq_ref.dtype)

    @pl.when(j == 4)
    def _():
        l0 = lb_ref[0:1, :]
        l1 = lb_ref[1:2, :]
        mx = jnp.maximum(l0, l1)
        e0 = jnp.exp(l0 - mx)
        e1 = jnp.exp(l1 - mx)
        lb = e1 / (e0 + e1)
        f = lb + (1.0 - lb) * jax.nn.sigmoid(acc)
        g_ref[...] = jnp.log(f)

    @pl.when(j == 5)
    def _():
        ri_ref[...] = acc.astype(ri_ref.dtype)

    @pl.when(j == 6)
    def _():
        rg_ref[...] = acc.astype(rg_ref.dtype)

    @pl.when(j == 7)
    def _():
        ga_ref[...] = acc.astype(ga_ref.dtype)

    @pl.when(j == 8)
    def _():
        gr_ref[...] = acc.astype(gr_ref.dtype)


def _inproj(x2d, ada, norm_w, w_in, pos, invf, lb_logits, *, seq, tm):
    t = x2d.shape[0]
    tiles_per_batch = seq // tm
    row = lambda i, j: (i, 0)
    const2 = lambda i, j: (0, 0)
    out_dtypes = (BF16, BF16, BF16, F32, F32, BF16, F32, F32, F32)
    return pl.pallas_call(
        _inproj_kernel,
        grid=(t // tm, N_SECTIONS),
        in_specs=[pl.BlockSpec((tm, D_MODEL), row),
                  pl.BlockSpec((1, 8, D_MODEL), lambda i, j: (i // tiles_per_batch, 0, 0)),
                  pl.BlockSpec((1, D_MODEL), const2),
                  pl.BlockSpec((D_MODEL, D_MODEL), lambda i, j: (0, j)),
                  pl.BlockSpec((tm, 1), row),
                  pl.BlockSpec((1, HEAD_W), const2),
                  pl.BlockSpec((2, D_MODEL), const2)],
        out_specs=[pl.BlockSpec((tm, D_MODEL), row) for _ in out_dtypes],
        out_shape=[jax.ShapeDtypeStruct((t, D_MODEL), dt) for dt in out_dtypes],
        scratch_shapes=[pltpu.VMEM((tm, D_MODEL), BF16),
                        pltpu.VMEM((tm, HEAD_W), F32),
                        pltpu.VMEM((tm, HEAD_W), F32),
                        pltpu.VMEM((tm, HEAD_W), F32)],
        compiler_params=pltpu.CompilerParams(
            dimension_semantics=("arbitrary", "arbitrary"), vmem_limit_bytes=VMEM_LIMIT),
        name="inproj",
    )(x2d, ada, norm_w, w_in, pos, invf, lb_logits)


def _attn_kernel(lq1_ref, lk1_ref, lq2_ref, lk2_ref, sw_ref, q_ref, k_ref, v_ref, o_ref,
                 acc1, acc2, m1, l1, m2, l2, *, tq):
    qi = pl.program_id(2)
    q = q_ref[0]
    lane = lax.broadcasted_iota(jnp.int32, (1, HEAD_W), 1)
    zero = jnp.zeros_like(q)
    q_maps = (jnp.where(lane < QK_DIM, q, zero), jnp.where(lane >= QK_DIM, q, zero))
    state = ((acc1, m1, l1), (acc2, m2, l2))
    for acc, m, l in state:
        acc[...] = jnp.zeros_like(acc)
        m[...] = jnp.full_like(m, -jnp.inf)
        l[...] = jnp.zeros_like(l)

    def step(kj, masked):
        start = pl.multiple_of(kj * tq, tq)
        k = k_ref[0, pl.ds(start, tq), :]
        v = v_ref[0, pl.ds(start, tq), :]
        for qm, (acc, m, l) in zip(q_maps, state):
            s = lax.dot_general(qm, k, _NT, preferred_element_type=F32)
            if masked:
                r = lax.broadcasted_iota(jnp.int32, s.shape, 0)
                c = lax.broadcasted_iota(jnp.int32, s.shape, 1)
                s = jnp.where(r >= c, s, -jnp.inf)
            m_prev = m[...]
            m_new = jnp.maximum(m_prev, jnp.max(s, axis=-1, keepdims=True))
            alpha = jnp.exp(m_prev - m_new)
            p = jnp.exp(s - m_new)
            l[...] = alpha * l[...] + jnp.sum(p, axis=-1, keepdims=True)
            acc[...] = alpha * acc[...] + jnp.dot(p.astype(BF16), v, preferred_element_type=F32)
            m[...] = m_new

    def body(kj, carry):
        step(kj, False)
        return carry

    lax.fori_loop(0, qi, body, 0)
    step(qi, True)

    lam = (jnp.exp(jnp.sum(lq1_ref[...] * lk1_ref[...], axis=-1, keepdims=True))
           - jnp.exp(jnp.sum(lq2_ref[...] * lk2_ref[...], axis=-1, keepdims=True)) + LAM_INIT)
    o = acc1[...] / l1[...] - lam * (acc2[...] / l2[...])
    o_ref[0] = (_rms(o) * sw_ref[...] * (1.0 - LAM_INIT)).astype(o_ref.dtype)


def _attn(q3, k3, v3, lq1, lk1, lq2, lk2, subln_w, *, tq):
    b, s, _ = q3.shape
    vec = lambda bi, h, qi: (0, 0)
    kernel = functools.partial(_attn_kernel, tq=tq)
    return pl.pallas_call(
        kernel,
        grid=(b, HEADS, s // tq),
        in_specs=[pl.BlockSpec((1, QK_DIM), vec)] * 4
                 + [pl.BlockSpec((1, HEAD_W), vec),
                    pl.BlockSpec((1, tq, HEAD_W), lambda bi, h, qi: (bi, qi, h)),
                    pl.BlockSpec((1, s, HEAD_W), lambda bi, h, qi: (bi, 0, h)),
                    pl.BlockSpec((1, s, HEAD_W), lambda bi, h, qi: (bi, 0, h))],
        out_specs=pl.BlockSpec((1, tq, HEAD_W), lambda bi, h, qi: (bi, qi, h)),
        out_shape=jax.ShapeDtypeStruct((b, s, D_MODEL), BF16),
        scratch_shapes=[pltpu.VMEM((tq, HEAD_W), F32), pltpu.VMEM((tq, HEAD_W), F32),
                        pltpu.VMEM((tq, 1), F32), pltpu.VMEM((tq, 1), F32),
                        pltpu.VMEM((tq, 1), F32), pltpu.VMEM((tq, 1), F32)],
        compiler_params=pltpu.CompilerParams(
            dimension_semantics=("arbitrary", "arbitrary", "arbitrary"), vmem_limit_bytes=VMEM_LIMIT),
        name="attn",
    )(lq1, lk1, lq2, lk2, subln_w, q3, k3, v3)


def _hgrn_kernel(rq_ref, g_ref, ri_ref, rg_ref, nw_ref, o_ref,
                 st_scr, sc_scr, *, n_chunks):
    c_len = REC_CHUNK

    @pl.when(pl.program_id(1) == 0)
    def _():
        st_scr[...] = jnp.zeros_like(st_scr)

    rows_i = lax.broadcasted_iota(jnp.int32, (c_len, c_len), 0)
    cols_i = lax.broadcasted_iota(jnp.int32, (c_len, c_len), 1)
    causal = rows_i >= cols_i
    tri = causal.astype(F32)
    row_col = lax.broadcasted_iota(jnp.int32, (c_len, 1), 0)

    for c in range(n_chunks):
        rows = slice(c * c_len, (c + 1) * c_len)
        g = g_ref[rows, :]
        b = jnp.dot(tri, g, preferred_element_type=F32, precision=lax.Precision.HIGHEST)
        b_last = b[c_len - 1:c_len, :]
        q = rq_ref[rows, :].astype(F32)
        kk = 1.0 - jnp.exp(g)
        qe = q * jnp.exp(b)
        kd = kk * jnp.exp(b_last - b)
        decay_last = jnp.exp(b_last)
        safe = jnp.max(-b_last) < SAFE_DECAY

        @pl.when(safe)
        def _():
            kh = (kk * jnp.exp(-b)).astype(BF16)
            qb = qe.astype(BF16)
            for h in range(HEADS):
                sl = slice(h * HEAD_W, (h + 1) * HEAD_W)
                s = lax.dot_general(qb[:, sl], kh[:, sl], _NT, preferred_element_type=F32)
                sc_scr[h] = jnp.where(causal, s, 0.0)

        @pl.when(jnp.logical_not(safe))
        def _():
            for h in range(HEADS):
                sl = slice(h * HEAD_W, (h + 1) * HEAD_W)
                qh, bh, kh = q[:, sl], b[:, sl], kk[:, sl]

                def col_body(s_idx, scores, qh=qh, bh=bh, kh=kh):
                    pick = row_col == s_idx
                    b_row = jnp.sum(jnp.where(pick, bh, 0.0), axis=0, keepdims=True)
                    k_row = jnp.sum(jnp.where(pick, kh, 0.0), axis=0, keepdims=True)
                    e = jnp.exp(jnp.minimum(bh - b_row, 0.0))
                    col = jnp.sum(qh * k_row * e, axis=1, keepdims=True)
                    col = jnp.where(row_col >= s_idx, col, 0.0)
                    return jnp.where(cols_i == s_idx, col, scores)

                sc_scr[h] = lax.fori_loop(0, c_len, col_body, jnp.zeros((c_len, c_len), F32))

        qb = qe.astype(BF16)
        kdb = kd.astype(BF16)
        for h in range(HEADS):
            sl = slice(h * HEAD_W, (h + 1) * HEAD_W)
            v = ri_ref[rows, sl]
            st = st_scr[h]
            o = (jnp.dot(sc_scr[h].astype(BF16), v, preferred_element_type=F32)
                 + lax.dot_general(qb[:, sl], st.astype(BF16), _NT, preferred_element_type=F32))
            st_scr[h] = st * decay_last[:, sl] + lax.dot_general(v, kdb[:, sl], _TN,
                                                                 preferred_element_type=F32)
            rg = rg_ref[rows, sl].astype(F32)
            o_ref[rows, sl] = (_rms(o) * nw_ref[...] * (rg * jax.nn.sigmoid(rg))).astype(o_ref.dtype)


def _hgrn(rq, g, ri, rg, norm_w, *, batch, seq, tt):
    t = rq.shape[0]
    tiles = seq // tt
    row = lambda bi, ti: (bi * tiles + ti, 0)
    kernel = functools.partial(_hgrn_kernel, n_chunks=tt // REC_CHUNK)
    return pl.pallas_call(
        kernel,
        grid=(batch, tiles),
        in_specs=[pl.BlockSpec((tt, D_MODEL), row)] * 4
                 + [pl.BlockSpec((1, HEAD_W), lambda bi, ti: (0, 0))],
        out_specs=pl.BlockSpec((tt, D_MODEL), row),
        out_shape=jax.ShapeDtypeStruct((t, D_MODEL), BF16),
        scratch_shapes=[pltpu.VMEM((HEADS, HEAD_W, HEAD_W), F32),
                        pltpu.VMEM((HEADS, REC_CHUNK, REC_CHUNK), F32)],
        compiler_params=pltpu.CompilerParams(
            dimension_semantics=("arbitrary", "arbitrary"), vmem_limit_bytes=VMEM_LIMIT),
        name="hgrn",
    )(rq, g, ri, rg, norm_w)


def _merge_kernel(oa_ref, or_ref, ga_ref, gr_ref, x_ref, ada_ref, wa_ref, wr_ref, wo_ref, o_ref):
    ya = jnp.dot(oa_ref[...], wa_ref[...], preferred_element_type=F32)
    yr = jnp.dot(or_ref[...], wr_ref[...], preferred_element_type=F32)
    y = (jax.nn.sigmoid(ga_ref[...].astype(F32)) * ya
         + jax.nn.sigmoid(gr_ref[...].astype(F32)) * yr)
    upd = jnp.dot(y.astype(BF16), wo_ref[...], preferred_element_type=F32)
    o_ref[...] = x_ref[...] + ada_ref[0][2:3] * upd


def _merge(oa, orr, ga, gr, x2d, ada, wa, wr, wo, *, seq, tm):
    t = x2d.shape[0]
    tiles_per_batch = seq // tm
    row = lambda i: (i, 0)
    wspec = pl.BlockSpec((D_MODEL, D_MODEL), lambda i: (0, 0))
    return pl.pallas_call(
        _merge_kernel,
        grid=(t // tm,),
        in_specs=[pl.BlockSpec((tm, D_MODEL), row)] * 5
                 + [pl.BlockSpec((1, 8, D_MODEL), lambda i: (i // tiles_per_batch, 0, 0)),
                    wspec, wspec, wspec],
        out_specs=pl.BlockSpec((tm, D_MODEL), row),
        out_shape=jax.ShapeDtypeStruct((t, D_MODEL), F32),
        compiler_params=pltpu.CompilerParams(
            dimension_semantics=("arbitrary",), vmem_limit_bytes=VMEM_LIMIT),
        name="merge",
    )(oa, orr, ga, gr, x2d, ada, wa, wr, wo)


def _mlp_kernel(x_ref, ada_ref, nw_ref, nf_ref, w1_ref, w2_ref, o_ref):
    x = x_ref[...]
    ada = ada_ref[0]
    h = (_rms(x) * nw_ref[...] * (1.0 + ada[4:5]) + ada[3:4]).astype(BF16)
    acc = jnp.zeros_like(x)
    for c in range(D_FF // D_MODEL):
        cols = slice(c * D_MODEL, (c + 1) * D_MODEL)
        u = jnp.dot(h, w1_ref[:, cols], preferred_element_type=F32)
        u = jnp.square(jnp.maximum(u, 0.0))
        acc = acc + jnp.dot(u.astype(BF16), w2_ref[cols, :], preferred_element_type=F32)
    o_ref[...] = _rms(x + ada[5:6] * acc) * nf_ref[...]


def _mlp(x1, ada, norm_w, norm_final, w1, w2, *, seq, tm):
    t = x1.shape[0]
    tiles_per_batch = seq // tm
    row = lambda i: (i, 0)
    const = lambda i: (0, 0)
    return pl.pallas_call(
        _mlp_kernel,
        grid=(t // tm,),
        in_specs=[pl.BlockSpec((tm, D_MODEL), row),
                  pl.BlockSpec((1, 8, D_MODEL), lambda i: (i // tiles_per_batch, 0, 0)),
                  pl.BlockSpec((1, D_MODEL), const),
                  pl.BlockSpec((1, D_MODEL), const),
                  pl.BlockSpec((D_MODEL, D_FF), const, pipeline_mode=pl.Buffered(1)),
                  pl.BlockSpec((D_FF, D_MODEL), const, pipeline_mode=pl.Buffered(1))],
        out_specs=pl.BlockSpec((tm, D_MODEL), row),
        out_shape=jax.ShapeDtypeStruct((t, D_MODEL), F32),
        compiler_params=pltpu.CompilerParams(
            dimension_semantics=("arbitrary",), vmem_limit_bytes=VMEM_LIMIT),
        name="mlp",
    )(x1, ada, norm_w, norm_final, w1, w2)


def _rope_inv_freq_lanes():
    inv_freq = ROPE_THETA ** (-jnp.arange(0, ROPE_DIM, 2, dtype=F32) / ROPE_DIM)
    lane = jnp.arange(HEAD_W) % QK_DIM
    table = jnp.where(lane < ROPE_DIM, inv_freq[lane % (ROPE_DIM // 2)], 0.0)
    return table.reshape(1, HEAD_W).astype(F32)


def kernel(x, c, positions, w_ada, b_ada, norm_mix, w_in, lam_q1, lam_k1, lam_q2, lam_k2, subln_w, lb_logits, rec_norm_w, w_proj_att, w_proj_rec, w_out, norm_mlp, w_mlp_in, w_mlp_out, norm_final):
    batch, seq, d = x.shape
    assert d == D_MODEL and w_ada.shape[0] == 1, "single-layer, d_model=1024 only"
    assert batch <= 8
    t = batch * seq
    tm = min(512, seq)
    tq = min(512, seq)
    tt = min(256, seq)
    assert seq % tm == 0 and seq % tq == 0 and seq % tt == 0 and tt % REC_CHUNK == 0

    x2d = x.reshape(t, D_MODEL)
    c_pad = jnp.zeros((8, D_MODEL), F32).at[:batch].set(c)
    ada = _ada(c_pad, w_ada[0], b_ada[0].reshape(1, -1))[:batch]
    ada = jnp.pad(ada.reshape(batch, N_ADA, D_MODEL), ((0, 0), (0, 8 - N_ADA), (0, 0)))

    q, k, v, rq, g, ri, rg, ga, gr = _inproj(
        x2d, ada, norm_mix[0].reshape(1, -1), w_in[0].astype(BF16),
        positions.reshape(t, 1), _rope_inv_freq_lanes(), lb_logits, seq=seq, tm=tm)

    as3 = lambda a: a.reshape(batch, seq, D_MODEL)
    o_a = _attn(as3(q), as3(k), as3(v), lam_q1, lam_k1, lam_q2, lam_k2, subln_w, tq=tq)
    o_r = _hgrn(rq, g, ri, rg, rec_norm_w, batch=batch, seq=seq, tt=tt)

    x1 = _merge(o_a.reshape(t, D_MODEL), o_r, ga, gr, x2d, ada,
                w_proj_att[0].astype(BF16), w_proj_rec[0].astype(BF16), w_out[0].astype(BF16),
                seq=seq, tm=tm)
    out = _mlp(x1, ada, norm_mlp[0].reshape(1, -1), norm_final.reshape(1, -1),
               w_mlp_in[0].astype(BF16), w_mlp_out[0].astype(BF16), seq=seq, tm=tm)
    return out.reshape(batch, seq, D_MODEL)
```

```python
import functools
import math

import jax
import jax.numpy as jnp
from jax import lax
from jax.experimental import pallas as pl
from jax.experimental.pallas import tpu as pltpu

F32 = jnp.float32
BF16 = jnp.bfloat16

D_MODEL = 1024
HEADS = 8
HEAD_W = 128
QK_DIM = 64
ROPE_DIM = 16
ROPE_THETA = 500000.0
D_FF = 4 * D_MODEL
N_ADA = 6
N_SECTIONS = 9
REC_CHUNK = 64
EPS = 1e-6
LAM_INIT = 0.8 - 0.6 * math.exp(-0.3 * 0)
LOG2_E = 1.0 / math.log(2.0)
SAFE_DECAY = 80.0
VMEM_LIMIT = 56 * 1024 * 1024

_NT = (((1,), (1,)), ((), ()))
_TN = (((0,), (0,)), ((), ()))


def _rms(x):
    return x * lax.rsqrt(jnp.mean(x * x, axis=-1, keepdims=True) + EPS)


def _ada_kernel(c_ref, w_ref, b_ref, o_ref):
    c = c_ref[...]
    cond = c * jax.nn.sigmoid(c)
    o_ref[...] = jnp.dot(cond, w_ref[...], preferred_element_type=F32,
                         precision=lax.Precision.HIGHEST) + b_ref[...]


def _ada(c_pad, w_ada, b_ada):
    n = w_ada.shape[1]
    return pl.pallas_call(
        _ada_kernel,
        grid=(n // D_MODEL,),
        in_specs=[pl.BlockSpec((8, D_MODEL), lambda j: (0, 0)),
                  pl.BlockSpec((D_MODEL, D_MODEL), lambda j: (0, j)),
                  pl.BlockSpec((1, D_MODEL), lambda j: (0, j))],
        out_specs=pl.BlockSpec((8, D_MODEL), lambda j: (0, j)),
        out_shape=jax.ShapeDtypeStruct((8, n), F32),
        name="ada",
    )(c_pad, w_ada, b_ada)


def _inproj_kernel(x_ref, ada_ref, nw_ref, w_ref, pos_ref, invf_ref, lb_ref,
                   q_ref, k_ref, v_ref, rq_ref, g_ref, ri_ref, rg_ref, ga_ref, gr_ref,
                   h_scr, cos_scr, s1_scr, s2_scr):
    j = pl.program_id(1)

    @pl.when(j == 0)
    def _():
        ada = ada_ref[0]
        h = _rms(x_ref[...]) * nw_ref[...] * (1.0 + ada[1:2]) + ada[0:1]
        h_scr[...] = h.astype(BF16)
        ang = pos_ref[...].astype(F32) * invf_ref[...]
        lane = lax.broadcasted_iota(jnp.int32, (1, HEAD_W), 1) % QK_DIM
        c = jnp.cos(ang)
        s = jnp.sin(ang)
        cos_scr[...] = jnp.where(lane < ROPE_DIM, c, 1.0)
        s1_scr[...] = jnp.where(lane < ROPE_DIM // 2, -s, 0.0)
        s2_scr[...] = jnp.where((lane >= ROPE_DIM // 2) & (lane < ROPE_DIM), s, 0.0)

    acc = jnp.dot(h_scr[...], w_ref[...], preferred_element_type=F32)

    def rope_store(ref, scale):
        cs, s1, s2 = cos_scr[...], s1_scr[...], s2_scr[...]
        for hh in range(HEADS):
            t = acc[:, hh * HEAD_W:(hh + 1) * HEAD_W]
            r = t * cs + pltpu.roll(t, HEAD_W - ROPE_DIM // 2, 1) * s1 + pltpu.roll(t, ROPE_DIM // 2, 1) * s2
            if scale != 1.0:
                r = r * scale
            ref[:, hh * HEAD_W:(hh + 1) * HEAD_W] = r.astype(ref.dtype)

    @pl.when(j == 0)
    def _():
        rope_store(q_ref, QK_DIM ** -0.5 * LOG2_E)

    @pl.when(j == 1)
    def _():
        rope_store(k_ref, 1.0)

    @pl.when(j == 2)
    def _():
        v_ref[...] = acc.astype(v_ref.dtype)

    @pl.when(j == 3)
    def _():
        rq_ref[...] = acc.astype(rq_ref.dtype)

    @pl.when(j == 4)
    def _():
        l0 = lb_ref[0:1, :]
        l1 = lb_ref[1:2, :]
        mx = jnp.maximum(l0, l1)
        e0 = jnp.exp(l0 - mx)
        e1 = jnp.exp(l1 - mx)
        lb = e1 / (e0 + e1)
        f = lb + (1.0 - lb) * jax.nn.sigmoid(acc)
        g_ref[...] = jnp.log(f)

    @pl.when(j == 5)
    def _():
        ri_ref[...] = acc.astype(ri_ref.dtype)

    @pl.when(j == 6)
    def _():
        rg_ref[...] = acc.astype(rg_ref.dtype)

    @pl.when(j == 7)
    def _():
        ga_ref[...] = acc.astype(ga_ref.dtype)

    @pl.when(j == 8)
    def _():
        gr_ref[...] = acc.astype(gr_ref.dtype)


def _inproj(x2d, ada, norm_w, w_in, pos, invf, lb_logits, *, seq, tm):
    t = x2d.shape[0]
    tiles_per_batch = seq // tm
    row = lambda i, j: (i, 0)
    const2 = lambda i, j: (0, 0)
    out_dtypes = (BF16, BF16, BF16, F32, F32, BF16, F32, F32, F32)
    return pl.pallas_call(
        _inproj_kernel,
        grid=(t // tm, N_SECTIONS),
        in_specs=[pl.BlockSpec((tm, D_MODEL), row),
                  pl.BlockSpec((1, 8, D_MODEL), lambda i, j: (i // tiles_per_batch, 0, 0)),
                  pl.BlockSpec((1, D_MODEL), const2),
                  pl.BlockSpec((D_MODEL, D_MODEL), lambda i, j: (0, j)),
                  pl.BlockSpec((tm, 1), row),
                  pl.BlockSpec((1, HEAD_W), const2),
                  pl.BlockSpec((2, D_MODEL), const2)],
        out_specs=[pl.BlockSpec((tm, D_MODEL), row) for _ in out_dtypes],
        out_shape=[jax.ShapeDtypeStruct((t, D_MODEL), dt) for dt in out_dtypes],
        scratch_shapes=[pltpu.VMEM((tm, D_MODEL), BF16),
                        pltpu.VMEM((tm, HEAD_W), F32),
                        pltpu.VMEM((tm, HEAD_W), F32),
                        pltpu.VMEM((tm, HEAD_W), F32)],
        compiler_params=pltpu.CompilerParams(
            dimension_semantics=("arbitrary", "arbitrary"), vmem_limit_bytes=VMEM_LIMIT),
        name="inproj",
    )(x2d, ada, norm_w, w_in, pos, invf, lb_logits)


def _attn_kernel(lq1_ref, lk1_ref, lq2_ref, lk2_ref, sw_ref, q_ref, k_ref, v_ref, o_ref,
                 vt_scr, acc1, acc2, m1, l1, m2, l2, *, tq):
    qi = pl.program_id(2)
    n_tiles = pl.num_programs(2)

    @pl.when(qi == 0)
    def _():
        def tr(j, carry):
            start = pl.multiple_of(j * tq, tq)
            vt_scr[:, pl.ds(start, tq)] = v_ref[0, pl.ds(start, tq), :].T
            return carry
        lax.fori_loop(0, n_tiles, tr, 0)

    q = q_ref[0]
    lane = lax.broadcasted_iota(jnp.int32, (1, HEAD_W), 1)
    zero = jnp.zeros_like(q)
    q_maps = (jnp.where(lane < QK_DIM, q, zero), jnp.where(lane >= QK_DIM, q, zero))
    state = ((acc1, m1, l1), (acc2, m2, l2))
    for acc, m, l in state:
        acc[...] = jnp.zeros_like(acc)
        m[...] = jnp.full_like(m, -jnp.inf)
        l[...] = jnp.zeros_like(l)

    def step(kj, masked):
        start = pl.multiple_of(kj * tq, tq)
        k = k_ref[0, pl.ds(start, tq), :]
        vt = vt_scr[:, pl.ds(start, tq)]
        for qm, (acc, m, l) in zip(q_maps, state):
            st = lax.dot_general(k, qm, _NT, preferred_element_type=F32)
            if masked:
                kr = lax.broadcasted_iota(jnp.int32, st.shape, 0)
                qc = lax.broadcasted_iota(jnp.int32, st.shape, 1)
                st = jnp.where(kr <= qc, st, -jnp.inf)
            m_prev = m[...]
            m_new = jnp.maximum(m_prev, jnp.max(st, axis=0, keepdims=True))
            alpha = jnp.exp2(m_prev - m_new)
            pt = jnp.exp2(st - m_new)
            l[...] = alpha * l[...] + jnp.sum(pt, axis=0, keepdims=True)
            acc[...] = alpha * acc[...] + jnp.dot(vt, pt.astype(BF16), preferred_element_type=F32)
            m[...] = m_new

    def body(t, carry):
        step(2 * t, False)
        step(2 * t + 1, False)
        return carry

    lax.fori_loop(0, qi // 2, body, 0)

    @pl.when(qi % 2 == 1)
    def _():
        step(qi - 1, False)

    step(qi, True)

    lam = (jnp.exp(jnp.sum(lq1_ref[...] * lk1_ref[...], axis=-1, keepdims=True))
           - jnp.exp(jnp.sum(lq2_ref[...] * lk2_ref[...], axis=-1, keepdims=True)) + LAM_INIT)
    ot = acc1[...] / l1[...] - lam * (acc2[...] / l2[...])
    inv = lax.rsqrt(jnp.mean(ot * ot, axis=0, keepdims=True) + EPS)
    ot = ot * inv * sw_ref[...] * (1.0 - LAM_INIT)
    o_ref[0] = ot.T.astype(o_ref.dtype)


def _attn(q3, k3, v3, lq1, lk1, lq2, lk2, subln_col, *, tq):
    b, s, _ = q3.shape
    vec = lambda bi, h, qi: (0, 0)
    kernel = functools.partial(_attn_kernel, tq=tq)
    return pl.pallas_call(
        kernel,
        grid=(b, HEADS, s // tq),
        in_specs=[pl.BlockSpec((1, QK_DIM), vec)] * 4
                 + [pl.BlockSpec((HEAD_W, 1), vec),
                    pl.BlockSpec((1, tq, HEAD_W), lambda bi, h, qi: (bi, qi, h)),
                    pl.BlockSpec((1, s, HEAD_W), lambda bi, h, qi: (bi, 0, h)),
                    pl.BlockSpec((1, s, HEAD_W), lambda bi, h, qi: (bi, 0, h))],
        out_specs=pl.BlockSpec((1, tq, HEAD_W), lambda bi, h, qi: (bi, qi, h)),
        out_shape=jax.ShapeDtypeStruct((b, s, D_MODEL), BF16),
        scratch_shapes=[pltpu.VMEM((HEAD_W, s), BF16),
                        pltpu.VMEM((HEAD_W, tq), F32), pltpu.VMEM((HEAD_W, tq), F32),
                        pltpu.VMEM((1, tq), F32), pltpu.VMEM((1, tq), F32),
                        pltpu.VMEM((1, tq), F32), pltpu.VMEM((1, tq), F32)],
        compiler_params=pltpu.CompilerParams(
            dimension_semantics=("arbitrary", "arbitrary", "arbitrary"), vmem_limit_bytes=VMEM_LIMIT),
        name="attn",
    )(lq1, lk1, lq2, lk2, subln_col, q3, k3, v3)


def _hgrn_kernel(rq_ref, g_ref, ri_ref, rg_ref, nw_ref, o_ref,
                 st_scr, sc_scr, *, n_chunks):
    c_len = REC_CHUNK

    @pl.when(pl.program_id(1) == 0)
    def _():
        st_scr[...] = jnp.zeros_like(st_scr)

    rows_i = lax.broadcasted_iota(jnp.int32, (c_len, c_len), 0)
    cols_i = lax.broadcasted_iota(jnp.int32, (c_len, c_len), 1)
    causal = rows_i >= cols_i
    tri = causal.astype(F32)
    row_col = lax.broadcasted_iota(jnp.int32, (c_len, 1), 0)

    for c in range(n_chunks):
        rows = slice(c * c_len, (c + 1) * c_len)
        g = g_ref[rows, :]
        b = jnp.dot(tri, g, preferred_element_type=F32, precision=lax.Precision.HIGHEST)
        b_last = b[c_len - 1:c_len, :]
        q = rq_ref[rows, :].astype(F32)
        kk = 1.0 - jnp.exp(g)
        qe = q * jnp.exp(b)
        kd = kk * jnp.exp(b_last - b)
        decay_last = jnp.exp(b_last)
        b_mid = b[c_len // 2 - 1:c_len // 2, :]
        safe = jnp.maximum(jnp.max(-b_mid), jnp.max(b_mid - b_last)) < SAFE_DECAY

        @pl.when(safe)
        def _():
            qb = (q * jnp.exp(b - b_mid)).astype(BF16)
            kh = (kk * jnp.exp(b_mid - b)).astype(BF16)
            for h in range(HEADS):
                sl = slice(h * HEAD_W, (h + 1) * HEAD_W)
                s = lax.dot_general(qb[:, sl], kh[:, sl], _NT, preferred_element_type=F32)
                sc_scr[h] = jnp.where(causal, s, 0.0)

        @pl.when(jnp.logical_not(safe))
        def _():
            for h in range(HEADS):
                sl = slice(h * HEAD_W, (h + 1) * HEAD_W)
                qh, bh, kh = q[:, sl], b[:, sl], kk[:, sl]

                def col_body(s_idx, scores, qh=qh, bh=bh, kh=kh):
                    pick = row_col == s_idx
                    b_row = jnp.sum(jnp.where(pick, bh, 0.0), axis=0, keepdims=True)
                    k_row = jnp.sum(jnp.where(pick, kh, 0.0), axis=0, keepdims=True)
                    e = jnp.exp(jnp.minimum(bh - b_row, 0.0))
                    col = jnp.sum(qh * k_row * e, axis=1, keepdims=True)
                    col = jnp.where(row_col >= s_idx, col, 0.0)
                    return jnp.where(cols_i == s_idx, col, scores)

                sc_scr[h] = lax.fori_loop(0, c_len, col_body, jnp.zeros((c_len, c_len), F32))

        qb = qe.astype(BF16)
        kdb = kd.astype(BF16)
        for h in range(HEADS):
            sl = slice(h * HEAD_W, (h + 1) * HEAD_W)
            v = ri_ref[rows, sl]
            st = st_scr[h]
            o = (jnp.dot(sc_scr[h].astype(BF16), v, preferred_element_type=F32)
                 + lax.dot_general(qb[:, sl], st.astype(BF16), _NT, preferred_element_type=F32))
            st_scr[h] = st * decay_last[:, sl] + lax.dot_general(v, kdb[:, sl], _TN,
                                                                 preferred_element_type=F32)
            rg = rg_ref[rows, sl].astype(F32)
            o_ref[rows, sl] = (_rms(o) * nw_ref[...] * (rg * jax.nn.sigmoid(rg))).astype(o_ref.dtype)


def _hgrn(rq, g, ri, rg, norm_w, *, batch, seq, tt):
    t = rq.shape[0]
    tiles = seq // tt
    row = lambda bi, ti: (bi * tiles + ti, 0)
    kernel = functools.partial(_hgrn_kernel, n_chunks=tt // REC_CHUNK)
    return pl.pallas_call(
        kernel,
        grid=(batch, tiles),
        in_specs=[pl.BlockSpec((tt, D_MODEL), row)] * 4
                 + [pl.BlockSpec((1, HEAD_W), lambda bi, ti: (0, 0))],
        out_specs=pl.BlockSpec((tt, D_MODEL), row),
        out_shape=jax.ShapeDtypeStruct((t, D_MODEL), BF16),
        scratch_shapes=[pltpu.VMEM((HEADS, HEAD_W, HEAD_W), F32),
                        pltpu.VMEM((HEADS, REC_CHUNK, REC_CHUNK), F32)],
        compiler_params=pltpu.CompilerParams(
            dimension_semantics=("arbitrary", "arbitrary"), vmem_limit_bytes=VMEM_LIMIT),
        name="hgrn",
    )(rq, g, ri, rg, norm_w)


def _merge_kernel(oa_ref, or_ref, ga_ref, gr_ref, x_ref, ada_ref, wa_ref, wr_ref, wo_ref, o_ref):
    ya = jnp.dot(oa_ref[...], wa_ref[...], preferred_element_type=F32)
    yr = jnp.dot(or_ref[...], wr_ref[...], preferred_element_type=F32)
    y = (jax.nn.sigmoid(ga_ref[...].astype(F32)) * ya
         + jax.nn.sigmoid(gr_ref[...].astype(F32)) * yr)
    upd = jnp.dot(y.astype(BF16), wo_ref[...], preferred_element_type=F32)
    o_ref[...] = x_ref[...] + ada_ref[0][2:3] * upd


def _merge(oa, orr, ga, gr, x2d, ada, wa, wr, wo, *, seq, tm):
    t = x2d.shape[0]
    tiles_per_batch = seq // tm
    row = lambda i: (i, 0)
    wspec = pl.BlockSpec((D_MODEL, D_MODEL), lambda i: (0, 0))
    return pl.pallas_call(
        _merge_kernel,
        grid=(t // tm,),
        in_specs=[pl.BlockSpec((tm, D_MODEL), row)] * 5
                 + [pl.BlockSpec((1, 8, D_MODEL), lambda i: (i // tiles_per_batch, 0, 0)),
                    wspec, wspec, wspec],
        out_specs=pl.BlockSpec((tm, D_MODEL), row),
        out_shape=jax.ShapeDtypeStruct((t, D_MODEL), F32),
        compiler_params=pltpu.CompilerParams(
            dimension_semantics=("arbitrary",), vmem_limit_bytes=VMEM_LIMIT),
        name="merge",
    )(oa, orr, ga, gr, x2d, ada, wa, wr, wo)


def _mlp_kernel(x_ref, ada_ref, nw_ref, nf_ref, w1_ref, w2_ref, o_ref):
    x = x_ref[...]
    ada = ada_ref[0]
    h = (_rms(x) * nw_ref[...] * (1.0 + ada[4:5]) + ada[3:4]).astype(BF16)
    acc = jnp.zeros_like(x)
    for c in range(D_FF // D_MODEL):
        cols = slice(c * D_MODEL, (c + 1) * D_MODEL)
        u = jnp.dot(h, w1_ref[:, cols], preferred_element_type=F32)
        u = jnp.square(jnp.maximum(u, 0.0))
        acc = acc + jnp.dot(u.astype(BF16), w2_ref[cols, :], preferred_element_type=F32)
    o_ref[...] = _rms(x + ada[5:6] * acc) * nf_ref[...]


def _mlp(x1, ada, norm_w, norm_final, w1, w2, *, seq, tm):
    t = x1.shape[0]
    tiles_per_batch = seq // tm
    row = lambda i: (i, 0)
    const = lambda i: (0, 0)
    return pl.pallas_call(
        _mlp_kernel,
        grid=(t // tm,),
        in_specs=[pl.BlockSpec((tm, D_MODEL), row),
                  pl.BlockSpec((1, 8, D_MODEL), lambda i: (i // tiles_per_batch, 0, 0)),
                  pl.BlockSpec((1, D_MODEL), const),
                  pl.BlockSpec((1, D_MODEL), const),
                  pl.BlockSpec((D_MODEL, D_FF), const, pipeline_mode=pl.Buffered(1)),
                  pl.BlockSpec((D_FF, D_MODEL), const, pipeline_mode=pl.Buffered(1))],
        out_specs=pl.BlockSpec((tm, D_MODEL), row),
        out_shape=jax.ShapeDtypeStruct((t, D_MODEL), F32),
        compiler_params=pltpu.CompilerParams(
            dimension_semantics=("arbitrary",), vmem_limit_bytes=VMEM_LIMIT),
        name="mlp",
    )(x1, ada, norm_w, norm_final, w1, w2)


def _rope_inv_freq_lanes():
    inv_freq = ROPE_THETA ** (-jnp.arange(0, ROPE_DIM, 2, dtype=F32) / ROPE_DIM)
    lane = jnp.arange(HEAD_W) % QK_DIM
    table = jnp.where(lane < ROPE_DIM, inv_freq[lane % (ROPE_DIM // 2)], 0.0)
    return table.reshape(1, HEAD_W).astype(F32)


def kernel(x, c, positions, w_ada, b_ada, norm_mix, w_in, lam_q1, lam_k1, lam_q2, lam_k2, subln_w, lb_logits, rec_norm_w, w_proj_att, w_proj_rec, w_out, norm_mlp, w_mlp_in, w_mlp_out, norm_final):
    batch, seq, d = x.shape
    assert d == D_MODEL and w_ada.shape[0] == 1, "single-layer, d_model=1024 only"
    assert batch <= 8
    t = batch * seq
    tm = min(512, seq)
    tq = min(512, seq)
    tt = min(256, seq)
    assert seq % tm == 0 and seq % tq == 0 and seq % tt == 0 and tt % REC_CHUNK == 0

    x2d = x.reshape(t, D_MODEL)
    c_pad = jnp.zeros((8, D_MODEL), F32).at[:batch].set(c)
    ada = _ada(c_pad, w_ada[0], b_ada[0].reshape(1, -1))[:batch]
    ada = jnp.pad(ada.reshape(batch, N_ADA, D_MODEL), ((0, 0), (0, 8 - N_ADA), (0, 0)))

    q, k, v, rq, g, ri, rg, ga, gr = _inproj(
        x2d, ada, norm_mix[0].reshape(1, -1), w_in[0].astype(BF16),
        positions.reshape(t, 1), _rope_inv_freq_lanes(), lb_logits, seq=seq, tm=tm)

    as3 = lambda a: a.reshape(batch, seq, D_MODEL)
    o_a = _attn(as3(q), as3(k), as3(v), lam_q1, lam_k1, lam_q2, lam_k2,
                subln_w.reshape(HEAD_W, 1), tq=tq)
    o_r = _hgrn(rq, g, ri, rg, rec_norm_w, batch=batch, seq=seq, tt=tt)

    x1 = _merge(o_a.reshape(t, D_MODEL), o_r, ga, gr, x2d, ada,
                w_proj_att[0].astype(BF16), w_proj_rec[0].astype(BF16), w_out[0].astype(BF16),
                seq=seq, tm=tm)
    out = _mlp(x1, ada, norm_mlp[0].reshape(1, -1), norm_final.reshape(1, -1),
               w_mlp_in[0].astype(BF16), w_mlp_out[0].astype(BF16), seq=seq, tm=tm)
    return out.reshape(batch, seq, D_MODEL)
```

```python
import functools
import math

import jax
import jax.numpy as jnp
from jax import lax
from jax.experimental import pallas as pl
from jax.experimental.pallas import tpu as pltpu

F32 = jnp.float32
BF16 = jnp.bfloat16

D_MODEL = 1024
HEADS = 8
HEAD_W = 128
QK_DIM = 64
ROPE_DIM = 16
ROPE_THETA = 500000.0
D_FF = 4 * D_MODEL
N_ADA = 6
N_SECTIONS = 9
REC_CHUNK = 64
EPS = 1e-6
LAM_INIT = 0.8 - 0.6 * math.exp(-0.3 * 0)
LOG2_E = 1.0 / math.log(2.0)
SAFE_DECAY = 80.0
MAX_RISE = 64.0
SUB_KEYS = 256
STREAM_TILES = 4
QK_AHEAD = 3
VMEM_LIMIT = 56 * 1024 * 1024

_NT = (((1,), (1,)), ((), ()))
_TN = (((0,), (0,)), ((), ()))


def _rms(x):
    return x * lax.rsqrt(jnp.mean(x * x, axis=-1, keepdims=True) + EPS)


def _ada_kernel(c_ref, w_ref, b_ref, o_ref):
    c = c_ref[...]
    cond = c * jax.nn.sigmoid(c)
    o_ref[...] = jnp.dot(cond, w_ref[...], preferred_element_type=F32,
                         precision=lax.Precision.HIGHEST) + b_ref[...]


def _ada(c_pad, w_ada, b_ada):
    n = w_ada.shape[1]
    return pl.pallas_call(
        _ada_kernel,
        grid=(n // D_MODEL,),
        in_specs=[pl.BlockSpec((8, D_MODEL), lambda j: (0, 0)),
                  pl.BlockSpec((D_MODEL, D_MODEL), lambda j: (0, j)),
                  pl.BlockSpec((1, D_MODEL), lambda j: (0, j))],
        out_specs=pl.BlockSpec((8, D_MODEL), lambda j: (0, j)),
        out_shape=jax.ShapeDtypeStruct((8, n), F32),
        name="ada",
    )(c_pad, w_ada, b_ada)


def _inproj_kernel(x_ref, ada_ref, nw_ref, w_ref, pos_ref, invf_ref, lb_ref,
                   q_ref, k_ref, v_ref, rq_ref, g_ref, ri_ref, rg_ref, ga_ref, gr_ref,
                   h_scr, cos_scr, s1_scr, s2_scr):
    j = pl.program_id(1)

    @pl.when(j == 0)
    def _():
        ada = ada_ref[0]
        h = _rms(x_ref[...]) * nw_ref[...] * (1.0 + ada[1:2]) + ada[0:1]
        h_scr[...] = h.astype(BF16)
        ang = pos_ref[...].astype(F32) * invf_ref[...]
        lane = lax.broadcasted_iota(jnp.int32, (1, HEAD_W), 1) % QK_DIM
        c = jnp.cos(ang)
        s = jnp.sin(ang)
        cos_scr[...] = jnp.where(lane < ROPE_DIM, c, 1.0)
        s1_scr[...] = jnp.where(lane < ROPE_DIM // 2, -s, 0.0)
        s2_scr[...] = jnp.where((lane >= ROPE_DIM // 2) & (lane < ROPE_DIM), s, 0.0)

    acc = jnp.dot(h_scr[...], w_ref[...], preferred_element_type=F32)

    def rope_store(ref, scale):
        cs, s1, s2 = cos_scr[...], s1_scr[...], s2_scr[...]
        for hh in range(HEADS):
            t = acc[:, hh * HEAD_W:(hh + 1) * HEAD_W]
            r = t * cs + pltpu.roll(t, HEAD_W - ROPE_DIM // 2, 1) * s1 + pltpu.roll(t, ROPE_DIM // 2, 1) * s2
            if scale != 1.0:
                r = r * scale
            ref[:, hh * HEAD_W:(hh + 1) * HEAD_W] = r.astype(ref.dtype)

    @pl.when(j == 0)
    def _():
        rope_store(q_ref, QK_DIM ** -0.5 * LOG2_E)

    @pl.when(j == 1)
    def _():
        rope_store(k_ref, 1.0)

    @pl.when(j == 2)
    def _():
        v_ref[...] = acc.astype(v_ref.dtype)

    @pl.when(j == 3)
    def _():
        rq_ref[...] = acc.astype(rq_ref.dtype)

    @pl.when(j == 4)
    def _():
        l0 = lb_ref[0:1, :]
        l1 = lb_ref[1:2, :]
        mx = jnp.maximum(l0, l1)
        e0 = jnp.exp(l0 - mx)
        e1 = jnp.exp(l1 - mx)
        lb = e1 / (e0 + e1)
        f = lb + (1.0 - lb) * jax.nn.sigmoid(acc)
        g_ref[...] = jnp.log(f)

    @pl.when(j == 5)
    def _():
        ri_ref[...] = acc.astype(ri_ref.dtype)

    @pl.when(j == 6)
    def _():
        rg_ref[...] = acc.astype(rg_ref.dtype)

    @pl.when(j == 7)
    def _():
        ga_ref[...] = acc.astype(ga_ref.dtype)

    @pl.when(j == 8)
    def _():
        gr_ref[...] = acc.astype(gr_ref.dtype)


def _inproj(x2d, ada, norm_w, w_in, pos, invf, lb_logits, *, seq, tm):
    t = x2d.shape[0]
    tiles_per_batch = seq // tm
    row = lambda i, j: (i, 0)
    const2 = lambda i, j: (0, 0)
    out_dtypes = (BF16, BF16, BF16, F32, F32, BF16, F32, F32, F32)
    return pl.pallas_call(
        _inproj_kernel,
        grid=(t // tm, N_SECTIONS),
        in_specs=[pl.BlockSpec((tm, D_MODEL), row),
                  pl.BlockSpec((1, 8, D_MODEL), lambda i, j: (i // tiles_per_batch, 0, 0)),
                  pl.BlockSpec((1, D_MODEL), const2),
                  pl.BlockSpec((D_MODEL, D_MODEL), lambda i, j: (0, j)),
                  pl.BlockSpec((tm, 1), row),
                  pl.BlockSpec((1, HEAD_W), const2),
                  pl.BlockSpec((2, D_MODEL), const2)],
        out_specs=[pl.BlockSpec((tm, D_MODEL), row) for _ in out_dtypes],
        out_shape=[jax.ShapeDtypeStruct((t, D_MODEL), dt) for dt in out_dtypes],
        scratch_shapes=[pltpu.VMEM((tm, D_MODEL), BF16),
                        pltpu.VMEM((tm, HEAD_W), F32),
                        pltpu.VMEM((tm, HEAD_W), F32),
                        pltpu.VMEM((tm, HEAD_W), F32)],
        compiler_params=pltpu.CompilerParams(
            dimension_semantics=("arbitrary", "arbitrary"), vmem_limit_bytes=VMEM_LIMIT),
        name="inproj",
    )(x2d, ada, norm_w, w_in, pos, invf, lb_logits)


def _attn_kernel(lq1_ref, lk1_ref, lq2_ref, lk2_ref, sw_ref, q_ref, k_ref, v_ref, o_ref,
                 vt_scr, acc1, acc2, m1, l1, m2, l2, dev1, dev2, *, tq):
    qi = pl.program_id(2)
    n_tiles = pl.num_programs(2)

    @pl.when(qi == 0)
    def _():
        def tr(j, carry):
            start = pl.multiple_of(j * tq, tq)
            vt_scr[:, pl.ds(start, tq)] = v_ref[0, pl.ds(start, tq), :].T
            return carry
        lax.fori_loop(0, n_tiles, tr, 0)

    q = q_ref[0]
    lane = lax.broadcasted_iota(jnp.int32, (1, HEAD_W), 1)
    zero = jnp.zeros_like(q)
    q_maps = (jnp.where(lane < QK_DIM, q, zero), jnp.where(lane >= QK_DIM, q, zero))
    state = ((acc1, m1, l1, dev1), (acc2, m2, l2, dev2))

    def init():
        for acc, m, l, dev in state:
            acc[...] = jnp.zeros_like(acc)
            m[...] = jnp.full_like(m, -jnp.inf)
            l[...] = jnp.zeros_like(l)
            dev[...] = jnp.zeros_like(dev)

    def tiles(kj):
        start = pl.multiple_of(kj * tq, tq)
        return k_ref[0, pl.ds(start, tq), :], vt_scr[:, pl.ds(start, tq)]

    def exact_step(kj):
        k, vt = tiles(kj)
        for qm, (acc, m, l, _) in zip(q_maps, state):
            st = lax.dot_general(k, qm, _NT, preferred_element_type=F32)
            kr = lax.broadcasted_iota(jnp.int32, st.shape, 0) + (kj - qi) * tq
            qc = lax.broadcasted_iota(jnp.int32, st.shape, 1)
            st = jnp.where(kr <= qc, st, -jnp.inf)
            m_prev = m[...]
            m_new = jnp.maximum(m_prev, jnp.max(st, axis=0, keepdims=True))
            alpha = jnp.exp2(m_prev - m_new)
            pt = jnp.exp2(st - m_new)
            l[...] = alpha * l[...] + jnp.sum(pt, axis=0, keepdims=True)
            acc[...] = alpha * acc[...] + jnp.dot(vt, pt.astype(BF16), preferred_element_type=F32)
            m[...] = m_new

    def stream_tiles(kj0, n):
        kv = [tiles(kj0 + t) for t in range(n)]
        n_sub = tq // SUB_KEYS
        chains = [(t, mi, h) for t in range(n) for mi in range(2) for h in range(n_sub)]

        def qk(c):
            t, mi, h = c
            ks = slice(h * SUB_KEYS, (h + 1) * SUB_KEYS)
            return lax.dot_general(kv[t][0][ks], q_maps[mi], _NT, preferred_element_type=F32)

        scores = {i: qk(c) for i, c in enumerate(chains[:QK_AHEAD])}
        part = {}
        for i, (t, mi, h) in enumerate(chains):
            acc, m, l, dev = state[mi]
            st = scores.pop(i)
            m_prev = m[...]
            pt = jnp.exp2(st - m_prev)
            mx = jnp.max(st, axis=0, keepdims=True)
            sm = jnp.sum(pt, axis=0, keepdims=True)
            if i + QK_AHEAD < len(chains):
                scores[i + QK_AHEAD] = qk(chains[i + QK_AHEAD])
            ks = slice(h * SUB_KEYS, (h + 1) * SUB_KEYS)
            pp = jnp.dot(kv[t][1][:, ks], pt.astype(BF16), preferred_element_type=F32)
            if h == 0:
                part[mi] = (mx, sm, pp)
            else:
                mx0, sm0, pp0 = part[mi]
                part[mi] = (jnp.maximum(mx0, mx), sm0 + sm, pp0 + pp)
            if h == n_sub - 1:
                t_max, l_add, pv = part.pop(mi)
                m_new = jnp.maximum(m_prev, t_max)
                alpha = jnp.exp2(m_prev - m_new)
                dev[...] = jnp.maximum(dev[...], t_max - m_prev)
                l[...] = alpha * (l[...] + l_add)
                acc[...] = alpha * (acc[...] + pv)
                m[...] = m_new

    init()
    exact_step(qi)

    def body(t, carry):
        stream_tiles(STREAM_TILES * t, STREAM_TILES)
        return carry

    lax.fori_loop(0, qi // STREAM_TILES, body, 0)

    for r in range(1, STREAM_TILES):
        @pl.when(qi % STREAM_TILES == r)
        def _(r=r):
            stream_tiles(qi - r, r)

    rise = jnp.maximum(jnp.max(dev1[...]), jnp.max(dev2[...]))

    @pl.when(jnp.logical_not(rise <= MAX_RISE))
    def _():
        init()

        def redo(kj, carry):
            exact_step(kj)
            return carry
        lax.fori_loop(0, qi + 1, redo, 0)

    lam = (jnp.exp(jnp.sum(lq1_ref[...] * lk1_ref[...], axis=-1, keepdims=True))
           - jnp.exp(jnp.sum(lq2_ref[...] * lk2_ref[...], axis=-1, keepdims=True)) + LAM_INIT)
    ot = acc1[...] / l1[...] - lam * (acc2[...] / l2[...])
    inv = lax.rsqrt(jnp.mean(ot * ot, axis=0, keepdims=True) + EPS)
    ot = ot * inv * sw_ref[...] * (1.0 - LAM_INIT)
    o_ref[0] = ot.T.astype(o_ref.dtype)


def _attn(q3, k3, v3, lq1, lk1, lq2, lk2, subln_col, *, tq):
    b, s, _ = q3.shape
    vec = lambda bi, h, qi: (0, 0)
    kernel = functools.partial(_attn_kernel, tq=tq)
    return pl.pallas_call(
        kernel,
        grid=(b, HEADS, s // tq),
        in_specs=[pl.BlockSpec((1, QK_DIM), vec)] * 4
                 + [pl.BlockSpec((HEAD_W, 1), vec),
                    pl.BlockSpec((1, tq, HEAD_W), lambda bi, h, qi: (bi, qi, h)),
                    pl.BlockSpec((1, s, HEAD_W), lambda bi, h, qi: (bi, 0, h)),
                    pl.BlockSpec((1, s, HEAD_W), lambda bi, h, qi: (bi, 0, h))],
        out_specs=pl.BlockSpec((1, tq, HEAD_W), lambda bi, h, qi: (bi, qi, h)),
        out_shape=jax.ShapeDtypeStruct((b, s, D_MODEL), BF16),
        scratch_shapes=[pltpu.VMEM((HEAD_W, s), BF16),
                        pltpu.VMEM((HEAD_W, tq), F32), pltpu.VMEM((HEAD_W, tq), F32),
                        pltpu.VMEM((1, tq), F32), pltpu.VMEM((1, tq), F32),
                        pltpu.VMEM((1, tq), F32), pltpu.VMEM((1, tq), F32),
                        pltpu.VMEM((1, tq), F32), pltpu.VMEM((1, tq), F32)],
        compiler_params=pltpu.CompilerParams(
            dimension_semantics=("arbitrary", "arbitrary", "arbitrary"), vmem_limit_bytes=VMEM_LIMIT),
        name="attn",
    )(lq1, lk1, lq2, lk2, subln_col, q3, k3, v3)


def _hgrn_kernel(rq_ref, g_ref, ri_ref, rg_ref, nw_ref, o_ref,
                 st_scr, sc_scr, *, n_chunks):
    c_len = REC_CHUNK

    @pl.when(pl.program_id(1) == 0)
    def _():
        st_scr[...] = jnp.zeros_like(st_scr)

    rows_i = lax.broadcasted_iota(jnp.int32, (c_len, c_len), 0)
    cols_i = lax.broadcasted_iota(jnp.int32, (c_len, c_len), 1)
    causal = rows_i >= cols_i
    tri = causal.astype(F32)
    row_col = lax.broadcasted_iota(jnp.int32, (c_len, 1), 0)

    for c in range(n_chunks):
        rows = slice(c * c_len, (c + 1) * c_len)
        g = g_ref[rows, :]
        b = jnp.dot(tri, g, preferred_element_type=F32, precision=lax.Precision.HIGHEST)
        b_last = b[c_len - 1:c_len, :]
        q = rq_ref[rows, :].astype(F32)
        kk = 1.0 - jnp.exp(g)
        qe = q * jnp.exp(b)
        kd = kk * jnp.exp(b_last - b)
        decay_last = jnp.exp(b_last)
        b_mid = b[c_len // 2 - 1:c_len // 2, :]
        safe = jnp.maximum(jnp.max(-b_mid), jnp.max(b_mid - b_last)) < SAFE_DECAY

        @pl.when(safe)
        def _():
            qb = (q * jnp.exp(b - b_mid)).astype(BF16)
            kh = (kk * jnp.exp(b_mid - b)).astype(BF16)
            for h in range(HEADS):
                sl = slice(h * HEAD_W, (h + 1) * HEAD_W)
                s = lax.dot_general(qb[:, sl], kh[:, sl], _NT, preferred_element_type=F32)
                sc_scr[h] = jnp.where(causal, s, 0.0)

        @pl.when(jnp.logical_not(safe))
        def _():
            for h in range(HEADS):
                sl = slice(h * HEAD_W, (h + 1) * HEAD_W)
                qh, bh, kh = q[:, sl], b[:, sl], kk[:, sl]

                def col_body(s_idx, scores, qh=qh, bh=bh, kh=kh):
                    pick = row_col == s_idx
                    b_row = jnp.sum(jnp.where(pick, bh, 0.0), axis=0, keepdims=True)
                    k_row = jnp.sum(jnp.where(pick, kh, 0.0), axis=0, keepdims=True)
                    e = jnp.exp(jnp.minimum(bh - b_row, 0.0))
                    col = jnp.sum(qh * k_row * e, axis=1, keepdims=True)
                    col = jnp.where(row_col >= s_idx, col, 0.0)
                    return jnp.where(cols_i == s_idx, col, scores)

                sc_scr[h] = lax.fori_loop(0, c_len, col_body, jnp.zeros((c_len, c_len), F32))

        qb = qe.astype(BF16)
        kdb = kd.astype(BF16)
        for h in range(HEADS):
            sl = slice(h * HEAD_W, (h + 1) * HEAD_W)
            v = ri_ref[rows, sl]
            st = st_scr[h]
            o = (jnp.dot(sc_scr[h].astype(BF16), v, preferred_element_type=F32)
                 + lax.dot_general(qb[:, sl], st.astype(BF16), _NT, preferred_element_type=F32))
            st_scr[h] = st * decay_last[:, sl] + lax.dot_general(v, kdb[:, sl], _TN,
                                                                 preferred_element_type=F32)
            rg = rg_ref[rows, sl].astype(F32)
            o_ref[rows, sl] = (_rms(o) * nw_ref[...] * (rg * jax.nn.sigmoid(rg))).astype(o_ref.dtype)


def _hgrn(rq, g, ri, rg, norm_w, *, batch, seq, tt):
    t = rq.shape[0]
    tiles = seq // tt
    row = lambda bi, ti: (bi * tiles + ti, 0)
    kernel = functools.partial(_hgrn_kernel, n_chunks=tt // REC_CHUNK)
    return pl.pallas_call(
        kernel,
        grid=(batch, tiles),
        in_specs=[pl.BlockSpec((tt, D_MODEL), row)] * 4
                 + [pl.BlockSpec((1, HEAD_W), lambda bi, ti: (0, 0))],
        out_specs=pl.BlockSpec((tt, D_MODEL), row),
        out_shape=jax.ShapeDtypeStruct((t, D_MODEL), BF16),
        scratch_shapes=[pltpu.VMEM((HEADS, HEAD_W, HEAD_W), F32),
                        pltpu.VMEM((HEADS, REC_CHUNK, REC_CHUNK), F32)],
        compiler_params=pltpu.CompilerParams(
            dimension_semantics=("arbitrary", "arbitrary"), vmem_limit_bytes=VMEM_LIMIT),
        name="hgrn",
    )(rq, g, ri, rg, norm_w)


def _merge_kernel(oa_ref, or_ref, ga_ref, gr_ref, x_ref, ada_ref, wa_ref, wr_ref, wo_ref, o_ref):
    ya = jnp.dot(oa_ref[...], wa_ref[...], preferred_element_type=F32)
    yr = jnp.dot(or_ref[...], wr_ref[...], preferred_element_type=F32)
    y = (jax.nn.sigmoid(ga_ref[...].astype(F32)) * ya
         + jax.nn.sigmoid(gr_ref[...].astype(F32)) * yr)
    upd = jnp.dot(y.astype(BF16), wo_ref[...], preferred_element_type=F32)
    o_ref[...] = x_ref[...] + ada_ref[0][2:3] * upd


def _merge(oa, orr, ga, gr, x2d, ada, wa, wr, wo, *, seq, tm):
    t = x2d.shape[0]
    tiles_per_batch = seq // tm
    row = lambda i: (i, 0)
    wspec = pl.BlockSpec((D_MODEL, D_MODEL), lambda i: (0, 0))
    return pl.pallas_call(
        _merge_kernel,
        grid=(t // tm,),
        in_specs=[pl.BlockSpec((tm, D_MODEL), row)] * 5
                 + [pl.BlockSpec((1, 8, D_MODEL), lambda i: (i // tiles_per_batch, 0, 0)),
                    wspec, wspec, wspec],
        out_specs=pl.BlockSpec((tm, D_MODEL), row),
        out_shape=jax.ShapeDtypeStruct((t, D_MODEL), F32),
        compiler_params=pltpu.CompilerParams(
            dimension_semantics=("arbitrary",), vmem_limit_bytes=VMEM_LIMIT),
        name="merge",
    )(oa, orr, ga, gr, x2d, ada, wa, wr, wo)


def _mlp_kernel(x_ref, ada_ref, nw_ref, nf_ref, w1_ref, w2_ref, o_ref):
    x = x_ref[...]
    ada = ada_ref[0]
    h = (_rms(x) * nw_ref[...] * (1.0 + ada[4:5]) + ada[3:4]).astype(BF16)
    acc = jnp.zeros_like(x)
    for c in range(D_FF // D_MODEL):
        cols = slice(c * D_MODEL, (c + 1) * D_MODEL)
        u = jnp.dot(h, w1_ref[:, cols], preferred_element_type=F32)
        u = jnp.square(jnp.maximum(u, 0.0))
        acc = acc + jnp.dot(u.astype(BF16), w2_ref[cols, :], preferred_element_type=F32)
    o_ref[...] = _rms(x + ada[5:6] * acc) * nf_ref[...]


def _mlp(x1, ada, norm_w, norm_final, w1, w2, *, seq, tm):
    t = x1.shape[0]
    tiles_per_batch = seq // tm
    row = lambda i: (i, 0)
    const = lambda i: (0, 0)
    return pl.pallas_call(
        _mlp_kernel,
        grid=(t // tm,),
        in_specs=[pl.BlockSpec((tm, D_MODEL), row),
                  pl.BlockSpec((1, 8, D_MODEL), lambda i: (i // tiles_per_batch, 0, 0)),
                  pl.BlockSpec((1, D_MODEL), const),
                  pl.BlockSpec((1, D_MODEL), const),
                  pl.BlockSpec((D_MODEL, D_FF), const, pipeline_mode=pl.Buffered(1)),
                  pl.BlockSpec((D_FF, D_MODEL), const, pipeline_mode=pl.Buffered(1))],
        out_specs=pl.BlockSpec((tm, D_MODEL), row),
        out_shape=jax.ShapeDtypeStruct((t, D_MODEL), F32),
        compiler_params=pltpu.CompilerParams(
            dimension_semantics=("arbitrary",), vmem_limit_bytes=VMEM_LIMIT),
        name="mlp",
    )(x1, ada, norm_w, norm_final, w1, w2)


def _rope_inv_freq_lanes():
    inv_freq = ROPE_THETA ** (-jnp.arange(0, ROPE_DIM, 2, dtype=F32) / ROPE_DIM)
    lane = jnp.arange(HEAD_W) % QK_DIM
    table = jnp.where(lane < ROPE_DIM, inv_freq[lane % (ROPE_DIM // 2)], 0.0)
    return table.reshape(1, HEAD_W).astype(F32)


def kernel(x, c, positions, w_ada, b_ada, norm_mix, w_in, lam_q1, lam_k1, lam_q2, lam_k2, subln_w, lb_logits, rec_norm_w, w_proj_att, w_proj_rec, w_out, norm_mlp, w_mlp_in, w_mlp_out, norm_final):
    batch, seq, d = x.shape
    assert d == D_MODEL and w_ada.shape[0] == 1, "single-layer, d_model=1024 only"
    assert batch <= 8
    t = batch * seq
    tm = min(512, seq)
    tq = min(512, seq)
    tt = min(256, seq)
    assert seq % tm == 0 and seq % tq == 0 and seq % tt == 0 and tt % REC_CHUNK == 0

    x2d = x.reshape(t, D_MODEL)
    c_pad = jnp.zeros((8, D_MODEL), F32).at[:batch].set(c)
    ada = _ada(c_pad, w_ada[0], b_ada[0].reshape(1, -1))[:batch]
    ada = jnp.pad(ada.reshape(batch, N_ADA, D_MODEL), ((0, 0), (0, 8 - N_ADA), (0, 0)))

    q, k, v, rq, g, ri, rg, ga, gr = _inproj(
        x2d, ada, norm_mix[0].reshape(1, -1), w_in[0].astype(BF16),
        positions.reshape(t, 1), _rope_inv_freq_lanes(), lb_logits, seq=seq, tm=tm)

    as3 = lambda a: a.reshape(batch, seq, D_MODEL)
    o_a = _attn(as3(q), as3(k), as3(v), lam_q1, lam_k1, lam_q2, lam_k2,
                subln_w.reshape(HEAD_W, 1), tq=tq)
    o_r = _hgrn(rq, g, ri, rg, rec_norm_w, batch=batch, seq=seq, tt=tt)

    x1 = _merge(o_a.reshape(t, D_MODEL), o_r, ga, gr, x2d, ada,
                w_proj_att[0].astype(BF16), w_proj_rec[0].astype(BF16), w_out[0].astype(BF16),
                seq=seq, tm=tm)
    out = _mlp(x1, ada, norm_mlp[0].reshape(1, -1), norm_final.reshape(1, -1),
               w_mlp_in[0].astype(BF16), w_mlp_out[0].astype(BF16), seq=seq, tm=tm)
    return out.reshape(batch, seq, D_MODEL)
```

```python
import functools
import math

import jax
import jax.numpy as jnp
from jax import lax
from jax.experimental import pallas as pl
from jax.experimental.pallas import tpu as pltpu

F32 = jnp.float32
BF16 = jnp.bfloat16

D_MODEL = 1024
HEADS = 8
HEAD_W = 128
QK_DIM = 64
ROPE_DIM = 16
ROPE_THETA = 500000.0
D_FF = 4 * D_MODEL
N_ADA = 6
N_SECTIONS = 9
REC_CHUNK = 64
EPS = 1e-6
LAM_INIT = 0.8 - 0.6 * math.exp(-0.3 * 0)
LOG2_E = 1.0 / math.log(2.0)
SAFE_DECAY = 80.0
MAX_RISE = 64.0
SUB_KEYS = 256
STREAM_TILES = 4
QK_AHEAD = 3
VMEM_LIMIT = 56 * 1024 * 1024

_NT = (((1,), (1,)), ((), ()))
_TN = (((0,), (0,)), ((), ()))


def _rms(x):
    return x * lax.rsqrt(jnp.mean(x * x, axis=-1, keepdims=True) + EPS)


def _ada_kernel(c_ref, w_ref, b_ref, o_ref):
    c = c_ref[...]
    cond = c * jax.nn.sigmoid(c)
    o_ref[...] = jnp.dot(cond, w_ref[...], preferred_element_type=F32,
                         precision=lax.Precision.HIGHEST) + b_ref[...]


def _ada(c_pad, w_ada, b_ada):
    n = w_ada.shape[1]
    return pl.pallas_call(
        _ada_kernel,
        grid=(n // D_MODEL,),
        in_specs=[pl.BlockSpec((8, D_MODEL), lambda j: (0, 0)),
                  pl.BlockSpec((D_MODEL, D_MODEL), lambda j: (0, j)),
                  pl.BlockSpec((1, D_MODEL), lambda j: (0, j))],
        out_specs=pl.BlockSpec((8, D_MODEL), lambda j: (0, j)),
        out_shape=jax.ShapeDtypeStruct((8, n), F32),
        name="ada",
    )(c_pad, w_ada, b_ada)


def _inproj_kernel(x_ref, ada_ref, nw_ref, w_ref, pos_ref, invf_ref, lb_ref,
                   q_ref, k_ref, v_ref, rq_ref, g_ref, ri_ref, rg_ref, ga_ref, gr_ref):
    ada = ada_ref[0]
    h = (_rms(x_ref[...]) * nw_ref[...] * (1.0 + ada[1:2]) + ada[0:1]).astype(BF16)

    ang = pos_ref[...].astype(F32) * invf_ref[...]
    lane = lax.broadcasted_iota(jnp.int32, (1, HEAD_W), 1) % QK_DIM
    cos_a = jnp.cos(ang)
    sin_a = jnp.sin(ang)
    cs = jnp.where(lane < ROPE_DIM, cos_a, 1.0)
    s1 = jnp.where(lane < ROPE_DIM // 2, -sin_a, 0.0)
    s2 = jnp.where((lane >= ROPE_DIM // 2) & (lane < ROPE_DIM), sin_a, 0.0)

    def section(j):
        return jnp.dot(h, w_ref[:, j * D_MODEL:(j + 1) * D_MODEL], preferred_element_type=F32)

    def rope_store(ref, acc, scale):
        c_, s1_, s2_ = cs * scale, s1 * scale, s2 * scale
        for hh in range(HEADS):
            t = acc[:, hh * HEAD_W:(hh + 1) * HEAD_W]
            r = (t * c_ + pltpu.roll(t, HEAD_W - ROPE_DIM // 2, 1) * s1_
                 + pltpu.roll(t, ROPE_DIM // 2, 1) * s2_)
            ref[:, hh * HEAD_W:(hh + 1) * HEAD_W] = r.astype(ref.dtype)

    rope_store(q_ref, section(0), QK_DIM ** -0.5 * LOG2_E)
    rope_store(k_ref, section(1), 1.0)
    v_ref[...] = section(2).astype(v_ref.dtype)
    rq_ref[...] = section(3).astype(rq_ref.dtype)

    l0 = lb_ref[0:1, :]
    l1 = lb_ref[1:2, :]
    mx = jnp.maximum(l0, l1)
    e0 = jnp.exp(l0 - mx)
    e1 = jnp.exp(l1 - mx)
    lb = e1 / (e0 + e1)
    g_ref[...] = jnp.log(lb + (1.0 - lb) * jax.nn.sigmoid(section(4)))

    ri_ref[...] = section(5).astype(ri_ref.dtype)
    rg_ref[...] = section(6).astype(rg_ref.dtype)
    ga_ref[...] = section(7).astype(ga_ref.dtype)
    gr_ref[...] = section(8).astype(gr_ref.dtype)


def _inproj(x2d, ada, norm_w, w_in, pos, invf, lb_logits, *, seq, tm):
    t = x2d.shape[0]
    tiles_per_batch = seq // tm
    row = lambda i: (i, 0)
    const2 = lambda i: (0, 0)
    out_dtypes = (BF16, BF16, BF16, BF16, F32, BF16, BF16, BF16, BF16)
    return pl.pallas_call(
        _inproj_kernel,
        grid=(t // tm,),
        in_specs=[pl.BlockSpec((tm, D_MODEL), row),
                  pl.BlockSpec((1, 8, D_MODEL), lambda i: (i // tiles_per_batch, 0, 0)),
                  pl.BlockSpec((1, D_MODEL), const2),
                  pl.BlockSpec((D_MODEL, N_SECTIONS * D_MODEL), const2, pipeline_mode=pl.Buffered(1)),
                  pl.BlockSpec((tm, 1), row),
                  pl.BlockSpec((1, HEAD_W), const2),
                  pl.BlockSpec((2, D_MODEL), const2)],
        out_specs=[pl.BlockSpec((tm, D_MODEL), row) for _ in out_dtypes],
        out_shape=[jax.ShapeDtypeStruct((t, D_MODEL), dt) for dt in out_dtypes],
        compiler_params=pltpu.CompilerParams(
            dimension_semantics=("arbitrary",), vmem_limit_bytes=VMEM_LIMIT),
        name="inproj",
    )(x2d, ada, norm_w, w_in, pos, invf, lb_logits)


def _attn_kernel(lq1_ref, lk1_ref, lq2_ref, lk2_ref, sw_ref, q_ref, k_ref, v_ref, o_ref,
                 vt_scr, acc1, acc2, m1, l1, m2, l2, dev1, dev2, *, tq):
    qi = pl.program_id(2)
    n_tiles = pl.num_programs(2)

    @pl.when(qi == 0)
    def _():
        def tr(j, carry):
            start = pl.multiple_of(j * tq, tq)
            vt_scr[:, pl.ds(start, tq)] = v_ref[0, pl.ds(start, tq), :].T
            return carry
        lax.fori_loop(0, n_tiles, tr, 0)

    q = q_ref[0]
    lane = lax.broadcasted_iota(jnp.int32, (1, HEAD_W), 1)
    zero = jnp.zeros_like(q)
    q_maps = (jnp.where(lane < QK_DIM, q, zero), jnp.where(lane >= QK_DIM, q, zero))
    state = ((acc1, m1, l1, dev1), (acc2, m2, l2, dev2))

    def init():
        for acc, m, l, dev in state:
            acc[...] = jnp.zeros_like(acc)
            m[...] = jnp.full_like(m, -jnp.inf)
            l[...] = jnp.zeros_like(l)
            dev[...] = jnp.zeros_like(dev)

    def tiles(kj):
        start = pl.multiple_of(kj * tq, tq)
        return k_ref[0, pl.ds(start, tq), :], vt_scr[:, pl.ds(start, tq)]

    def exact_step(kj):
        k, vt = tiles(kj)
        for qm, (acc, m, l, _) in zip(q_maps, state):
            st = lax.dot_general(k, qm, _NT, preferred_element_type=F32)
            kr = lax.broadcasted_iota(jnp.int32, st.shape, 0) + (kj - qi) * tq
            qc = lax.broadcasted_iota(jnp.int32, st.shape, 1)
            st = jnp.where(kr <= qc, st, -jnp.inf)
            m_prev = m[...]
            m_new = jnp.maximum(m_prev, jnp.max(st, axis=0, keepdims=True))
            alpha = jnp.exp2(m_prev - m_new)
            pt = jnp.exp2(st - m_new)
            l[...] = alpha * l[...] + jnp.sum(pt, axis=0, keepdims=True)
            acc[...] = alpha * acc[...] + jnp.dot(vt, pt.astype(BF16), preferred_element_type=F32)
            m[...] = m_new

    def diag_step():
        k, vt = tiles(qi)
        half = tq // 2
        problems = ((slice(0, half), slice(0, half)), (slice(0, tq), slice(half, tq)))
        sts = {(mi, pi): lax.dot_general(k[ks], q_maps[mi][qs], _NT, preferred_element_type=F32)
               for mi in range(2) for pi, (ks, qs) in enumerate(problems)}
        for mi, (acc, m, l, dev) in enumerate(state):
            dev[...] = jnp.zeros_like(dev)
            for pi, (ks, qs) in enumerate(problems):
                st = sts[mi, pi]
                kr = lax.broadcasted_iota(jnp.int32, st.shape, 0)
                qc = lax.broadcasted_iota(jnp.int32, st.shape, 1) + qs.start
                st = jnp.where(kr <= qc, st, -jnp.inf)
                m_new = jnp.max(st, axis=0, keepdims=True)
                pt = jnp.exp2(st - m_new)
                l[:, qs] = jnp.sum(pt, axis=0, keepdims=True)
                acc[:, qs] = jnp.dot(vt[:, ks], pt.astype(BF16), preferred_element_type=F32)
                m[:, qs] = m_new

    def stream_tiles(kj0, n):
        kv = [tiles(kj0 + t) for t in range(n)]
        n_sub = tq // SUB_KEYS
        chains = [(t, mi, h) for t in range(n) for mi in range(2) for h in range(n_sub)]

        def qk(c):
            t, mi, h = c
            ks = slice(h * SUB_KEYS, (h + 1) * SUB_KEYS)
            return lax.dot_general(kv[t][0][ks], q_maps[mi], _NT, preferred_element_type=F32)

        scores = {i: qk(c) for i, c in enumerate(chains[:QK_AHEAD])}
        part = {}
        for i, (t, mi, h) in enumerate(chains):
            acc, m, l, dev = state[mi]
            st = scores.pop(i)
            m_prev = m[...]
            pt = jnp.exp2(st - m_prev)
            mx = jnp.max(st, axis=0, keepdims=True)
            sm = jnp.sum(pt, axis=0, keepdims=True)
            if i + QK_AHEAD < len(chains):
                scores[i + QK_AHEAD] = qk(chains[i + QK_AHEAD])
            ks = slice(h * SUB_KEYS, (h + 1) * SUB_KEYS)
            pp = jnp.dot(kv[t][1][:, ks], pt.astype(BF16), preferred_element_type=F32)
            if h == 0:
                part[mi] = (mx, sm, pp)
            else:
                mx0, sm0, pp0 = part[mi]
                part[mi] = (jnp.maximum(mx0, mx), sm0 + sm, pp0 + pp)
            if h == n_sub - 1:
                t_max, l_add, pv = part.pop(mi)
                m_new = jnp.maximum(m_prev, t_max)
                alpha = jnp.exp2(m_prev - m_new)
                dev[...] = jnp.maximum(dev[...], t_max - m_prev)
                l[...] = alpha * (l[...] + l_add)
                acc[...] = alpha * (acc[...] + pv)
                m[...] = m_new

    diag_step()

    def body(t, carry):
        stream_tiles(STREAM_TILES * t, STREAM_TILES)
        return carry

    lax.fori_loop(0, qi // STREAM_TILES, body, 0)

    for r in range(1, STREAM_TILES):
        @pl.when(qi % STREAM_TILES == r)
        def _(r=r):
            stream_tiles(qi - r, r)

    rise = jnp.maximum(jnp.max(dev1[...]), jnp.max(dev2[...]))

    @pl.when(jnp.logical_not(rise <= MAX_RISE))
    def _():
        init()

        def redo(kj, carry):
            exact_step(kj)
            return carry
        lax.fori_loop(0, qi + 1, redo, 0)

    lam = (jnp.exp(jnp.sum(lq1_ref[...] * lk1_ref[...], axis=-1, keepdims=True))
           - jnp.exp(jnp.sum(lq2_ref[...] * lk2_ref[...], axis=-1, keepdims=True)) + LAM_INIT)
    ot = acc1[...] / l1[...] - lam * (acc2[...] / l2[...])
    inv = lax.rsqrt(jnp.mean(ot * ot, axis=0, keepdims=True) + EPS)
    ot = ot * inv * sw_ref[...] * (1.0 - LAM_INIT)
    o_ref[0] = ot.T.astype(o_ref.dtype)


def _attn(q3, k3, v3, lq1, lk1, lq2, lk2, subln_col, *, tq):
    b, s, _ = q3.shape
    vec = lambda bi, h, qi: (0, 0)
    kernel = functools.partial(_attn_kernel, tq=tq)
    return pl.pallas_call(
        kernel,
        grid=(b, HEADS, s // tq),
        in_specs=[pl.BlockSpec((1, QK_DIM), vec)] * 4
                 + [pl.BlockSpec((HEAD_W, 1), vec),
                    pl.BlockSpec((1, tq, HEAD_W), lambda bi, h, qi: (bi, qi, h)),
                    pl.BlockSpec((1, s, HEAD_W), lambda bi, h, qi: (bi, 0, h)),
                    pl.BlockSpec((1, s, HEAD_W), lambda bi, h, qi: (bi, 0, h))],
        out_specs=pl.BlockSpec((1, tq, HEAD_W), lambda bi, h, qi: (bi, qi, h)),
        out_shape=jax.ShapeDtypeStruct((b, s, D_MODEL), BF16),
        scratch_shapes=[pltpu.VMEM((HEAD_W, s), BF16),
                        pltpu.VMEM((HEAD_W, tq), F32), pltpu.VMEM((HEAD_W, tq), F32),
                        pltpu.VMEM((1, tq), F32), pltpu.VMEM((1, tq), F32),
                        pltpu.VMEM((1, tq), F32), pltpu.VMEM((1, tq), F32),
                        pltpu.VMEM((1, tq), F32), pltpu.VMEM((1, tq), F32)],
        compiler_params=pltpu.CompilerParams(
            dimension_semantics=("arbitrary", "arbitrary", "arbitrary"), vmem_limit_bytes=VMEM_LIMIT),
        name="attn",
    )(lq1, lk1, lq2, lk2, subln_col, q3, k3, v3)


def _hgrn_kernel(rq_ref, g_ref, ri_ref, rg_ref, nw_ref, o_ref,
                 st_scr, sc_scr, *, n_chunks):
    c_len = REC_CHUNK

    @pl.when(pl.program_id(1) == 0)
    def _():
        st_scr[...] = jnp.zeros_like(st_scr)

    rows_i = lax.broadcasted_iota(jnp.int32, (c_len, c_len), 0)
    cols_i = lax.broadcasted_iota(jnp.int32, (c_len, c_len), 1)
    causal = rows_i >= cols_i
    row_col = lax.broadcasted_iota(jnp.int32, (c_len, 1), 0)

    n_rows = n_chunks * c_len
    t_row = lax.broadcasted_iota(jnp.int32, (n_rows, n_rows), 0)
    t_col = lax.broadcasted_iota(jnp.int32, (n_rows, n_rows), 1)
    tri = ((t_row // c_len == t_col // c_len) & (t_row >= t_col)).astype(BF16)
    g_all = g_ref[...]
    g_hi = g_all.astype(BF16)
    rest = g_all - g_hi.astype(F32)
    g_mid = rest.astype(BF16)
    g_lo = (rest - g_mid.astype(F32)).astype(BF16)
    b_all = (jnp.dot(tri, g_hi, preferred_element_type=F32)
             + jnp.dot(tri, g_mid, preferred_element_type=F32)
             + jnp.dot(tri, g_lo, preferred_element_type=F32))

    def prepare(c):
        rows = slice(c * c_len, (c + 1) * c_len)
        g = g_all[rows]
        b = b_all[rows]
        b_last = b[c_len - 1:c_len, :]
        b_mid = b[c_len // 2 - 1:c_len // 2, :]
        safe = jnp.maximum(jnp.max(-b_mid), jnp.max(b_mid - b_last)) < SAFE_DECAY
        return rows, g, b, b_last, b_mid, safe

    def factored_scores(q, kk, b, b_mid):
        qb = (q * jnp.exp(b - b_mid)).astype(BF16)
        kh = (kk * jnp.exp(b_mid - b)).astype(BF16)

        def head(h):
            sl = slice(h * HEAD_W, (h + 1) * HEAD_W)
            s = lax.dot_general(qb[:, sl], kh[:, sl], _NT, preferred_element_type=F32)
            return jnp.where(causal, s, 0.0)
        return head

    def exact_scores(q, kk, b):
        def head(h):
            sl = slice(h * HEAD_W, (h + 1) * HEAD_W)
            qh, bh, kh = q[:, sl], b[:, sl], kk[:, sl]

            def col_body(s_idx, scores):
                pick = row_col == s_idx
                b_row = jnp.sum(jnp.where(pick, bh, 0.0), axis=0, keepdims=True)
                k_row = jnp.sum(jnp.where(pick, kh, 0.0), axis=0, keepdims=True)
                e = jnp.exp(jnp.minimum(bh - b_row, 0.0))
                col = jnp.sum(qh * k_row * e, axis=1, keepdims=True)
                col = jnp.where(row_col >= s_idx, col, 0.0)
                return jnp.where(cols_i == s_idx, col, scores)

            return lax.fori_loop(0, c_len, col_body, jnp.zeros((c_len, c_len), F32))
        return head

    def finish(rows, q, kk, b, b_last, scores_of_head):
        qb = (q * jnp.exp(b)).astype(BF16)
        kdb = (kk * jnp.exp(b_last - b)).astype(BF16)
        decay_last = jnp.exp(b_last)
        for h in range(HEADS):
            sl = slice(h * HEAD_W, (h + 1) * HEAD_W)
            v = ri_ref[rows, sl]
            st = st_scr[h]
            o = (jnp.dot(scores_of_head(h).astype(BF16), v, preferred_element_type=F32)
                 + lax.dot_general(qb[:, sl], st.astype(BF16), _NT, preferred_element_type=F32))
            st_scr[h] = st * decay_last[:, sl] + lax.dot_general(v, kdb[:, sl], _TN,
                                                                 preferred_element_type=F32)
            rg = rg_ref[rows, sl].astype(F32)
            o_ref[rows, sl] = (_rms(o) * nw_ref[...] * (rg * jax.nn.sigmoid(rg))).astype(o_ref.dtype)

    prepared = [prepare(c) for c in range(n_chunks)]
    all_safe = functools.reduce(jnp.logical_and, [p[5] for p in prepared])

    @pl.when(all_safe)
    def _():
        for rows, g, b, b_last, b_mid, _ in prepared:
            q = rq_ref[rows, :].astype(F32)
            kk = 1.0 - jnp.exp(g)
            finish(rows, q, kk, b, b_last, factored_scores(q, kk, b, b_mid))

    @pl.when(jnp.logical_not(all_safe))
    def _():
        for rows, g, b, b_last, b_mid, safe in prepared:
            q = rq_ref[rows, :].astype(F32)
            kk = 1.0 - jnp.exp(g)

            @pl.when(safe)
            def _():
                head = factored_scores(q, kk, b, b_mid)
                for h in range(HEADS):
                    sc_scr[h] = head(h)

            @pl.when(jnp.logical_not(safe))
            def _():
                head = exact_scores(q, kk, b)
                for h in range(HEADS):
                    sc_scr[h] = head(h)

            finish(rows, q, kk, b, b_last, lambda h: sc_scr[h])


def _hgrn(rq, g, ri, rg, norm_w, *, batch, seq, tt):
    t = rq.shape[0]
    tiles = seq // tt
    row = lambda bi, ti: (bi * tiles + ti, 0)
    kernel = functools.partial(_hgrn_kernel, n_chunks=tt // REC_CHUNK)
    return pl.pallas_call(
        kernel,
        grid=(batch, tiles),
        in_specs=[pl.BlockSpec((tt, D_MODEL), row)] * 4
                 + [pl.BlockSpec((1, HEAD_W), lambda bi, ti: (0, 0))],
        out_specs=pl.BlockSpec((tt, D_MODEL), row),
        out_shape=jax.ShapeDtypeStruct((t, D_MODEL), BF16),
        scratch_shapes=[pltpu.VMEM((HEADS, HEAD_W, HEAD_W), F32),
                        pltpu.VMEM((HEADS, REC_CHUNK, REC_CHUNK), F32)],
        compiler_params=pltpu.CompilerParams(
            dimension_semantics=("arbitrary", "arbitrary"), vmem_limit_bytes=VMEM_LIMIT),
        name="hgrn",
    )(rq, g, ri, rg, norm_w)


def _merge_kernel(oa_ref, or_ref, ga_ref, gr_ref, x_ref, ada_ref, wa_ref, wr_ref, wo_ref, o_ref):
    ya = jnp.dot(oa_ref[...], wa_ref[...], preferred_element_type=F32)
    yr = jnp.dot(or_ref[...], wr_ref[...], preferred_element_type=F32)
    y = (jax.nn.sigmoid(ga_ref[...].astype(F32)) * ya
         + jax.nn.sigmoid(gr_ref[...].astype(F32)) * yr)
    upd = jnp.dot(y.astype(BF16), wo_ref[...], preferred_element_type=F32)
    o_ref[...] = x_ref[...] + ada_ref[0][2:3] * upd


def _merge(oa, orr, ga, gr, x2d, ada, wa, wr, wo, *, seq, tm):
    t = x2d.shape[0]
    tiles_per_batch = seq // tm
    row = lambda i: (i, 0)
    wspec = pl.BlockSpec((D_MODEL, D_MODEL), lambda i: (0, 0))
    return pl.pallas_call(
        _merge_kernel,
        grid=(t // tm,),
        in_specs=[pl.BlockSpec((tm, D_MODEL), row)] * 5
                 + [pl.BlockSpec((1, 8, D_MODEL), lambda i: (i // tiles_per_batch, 0, 0)),
                    wspec, wspec, wspec],
        out_specs=pl.BlockSpec((tm, D_MODEL), row),
        out_shape=jax.ShapeDtypeStruct((t, D_MODEL), F32),
        compiler_params=pltpu.CompilerParams(
            dimension_semantics=("arbitrary",), vmem_limit_bytes=VMEM_LIMIT),
        name="merge",
    )(oa, orr, ga, gr, x2d, ada, wa, wr, wo)


def _mlp_kernel(x_ref, ada_ref, nw_ref, nf_ref, w1_ref, w2_ref, o_ref):
    x = x_ref[...]
    ada = ada_ref[0]
    h = (_rms(x) * nw_ref[...] * (1.0 + ada[4:5]) + ada[3:4]).astype(BF16)
    acc = jnp.zeros_like(x)
    for c in range(D_FF // D_MODEL):
        cols = slice(c * D_MODEL, (c + 1) * D_MODEL)
        u = jnp.dot(h, w1_ref[:, cols], preferred_element_type=F32)
        u = jnp.square(jnp.maximum(u, 0.0))
        acc = acc + jnp.dot(u.astype(BF16), w2_ref[cols, :], preferred_element_type=F32)
    o_ref[...] = _rms(x + ada[5:6] * acc) * nf_ref[...]


def _mlp(x1, ada, norm_w, norm_final, w1, w2, *, seq, tm):
    t = x1.shape[0]
    tiles_per_batch = seq // tm
    row = lambda i: (i, 0)
    const = lambda i: (0, 0)
    return pl.pallas_call(
        _mlp_kernel,
        grid=(t // tm,),
        in_specs=[pl.BlockSpec((tm, D_MODEL), row),
                  pl.BlockSpec((1, 8, D_MODEL), lambda i: (i // tiles_per_batch, 0, 0)),
                  pl.BlockSpec((1, D_MODEL), const),
                  pl.BlockSpec((1, D_MODEL), const),
                  pl.BlockSpec((D_MODEL, D_FF), const, pipeline_mode=pl.Buffered(1)),
                  pl.BlockSpec((D_FF, D_MODEL), const, pipeline_mode=pl.Buffered(1))],
        out_specs=pl.BlockSpec((tm, D_MODEL), row),
        out_shape=jax.ShapeDtypeStruct((t, D_MODEL), F32),
        compiler_params=pltpu.CompilerParams(
            dimension_semantics=("arbitrary",), vmem_limit_bytes=VMEM_LIMIT),
        name="mlp",
    )(x1, ada, norm_w, norm_final, w1, w2)


def _rope_inv_freq_lanes():
    inv_freq = ROPE_THETA ** (-jnp.arange(0, ROPE_DIM, 2, dtype=F32) / ROPE_DIM)
    lane = jnp.arange(HEAD_W) % QK_DIM
    table = jnp.where(lane < ROPE_DIM, inv_freq[lane % (ROPE_DIM // 2)], 0.0)
    return table.reshape(1, HEAD_W).astype(F32)


def kernel(x, c, positions, w_ada, b_ada, norm_mix, w_in, lam_q1, lam_k1, lam_q2, lam_k2, subln_w, lb_logits, rec_norm_w, w_proj_att, w_proj_rec, w_out, norm_mlp, w_mlp_in, w_mlp_out, norm_final):
    batch, seq, d = x.shape
    assert d == D_MODEL and w_ada.shape[0] == 1, "single-layer, d_model=1024 only"
    assert batch <= 8
    t = batch * seq
    tm = min(512, seq)
    tq = min(512, seq)
    tt = min(256, seq)
    assert seq % tm == 0 and seq % tq == 0 and seq % tt == 0 and tt % REC_CHUNK == 0

    x2d = x.reshape(t, D_MODEL)
    c_pad = jnp.zeros((8, D_MODEL), F32).at[:batch].set(c)
    ada = _ada(c_pad, w_ada[0], b_ada[0].reshape(1, -1))[:batch]
    ada = jnp.pad(ada.reshape(batch, N_ADA, D_MODEL), ((0, 0), (0, 8 - N_ADA), (0, 0)))

    q, k, v, rq, g, ri, rg, ga, gr = _inproj(
        x2d, ada, norm_mix[0].reshape(1, -1), w_in[0].astype(BF16),
        positions.reshape(t, 1), _rope_inv_freq_lanes(), lb_logits, seq=seq, tm=tm)

    as3 = lambda a: a.reshape(batch, seq, D_MODEL)
    o_a = _attn(as3(q), as3(k), as3(v), lam_q1, lam_k1, lam_q2, lam_k2,
                subln_w.reshape(HEAD_W, 1), tq=tq)
    o_r = _hgrn(rq, g, ri, rg, rec_norm_w, batch=batch, seq=seq, tt=tt)

    x1 = _merge(o_a.reshape(t, D_MODEL), o_r, ga, gr, x2d, ada,
                w_proj_att[0].astype(BF16), w_proj_rec[0].astype(BF16), w_out[0].astype(BF16),
                seq=seq, tm=tm)
    out = _mlp(x1, ada, norm_mlp[0].reshape(1, -1), norm_final.reshape(1, -1),
               w_mlp_in[0].astype(BF16), w_mlp_out[0].astype(BF16), seq=seq, tm=tm)
    return out.reshape(batch, seq, D_MODEL)
```

```python
import functools
import math

import jax
import jax.numpy as jnp
from jax import lax
from jax.experimental import pallas as pl
from jax.experimental.pallas import tpu as pltpu

F32 = jnp.float32
BF16 = jnp.bfloat16

D_MODEL = 1024
HEADS = 8
HEAD_W = 128
QK_DIM = 64
ROPE_DIM = 16
ROPE_THETA = 500000.0
D_FF = 4 * D_MODEL
N_ADA = 6
N_SECTIONS = 9
REC_CHUNK = 64
EPS = 1e-6
LAM_INIT = 0.8 - 0.6 * math.exp(-0.3 * 0)
LOG2_E = 1.0 / math.log(2.0)
SAFE_DECAY = 115.0
MAX_RISE = 64.0
SUB_KEYS = 256
STREAM_TILES = 4
QK_AHEAD = 2
HEADS_PER_STEP = 2
VMEM_LIMIT = 56 * 1024 * 1024

_NT = (((1,), (1,)), ((), ()))
_TN = (((0,), (0,)), ((), ()))


def _rms(x):
    return x * lax.rsqrt(jnp.mean(x * x, axis=-1, keepdims=True) + EPS)


def _ada_kernel(c_ref, w_ref, b_ref, o_ref):
    c = c_ref[...]
    cond = c * jax.nn.sigmoid(c)
    o_ref[...] = jnp.dot(cond, w_ref[...], preferred_element_type=F32,
                         precision=lax.Precision.HIGHEST) + b_ref[...]


def _ada(c_pad, w_ada, b_ada):
    n = w_ada.shape[1]
    return pl.pallas_call(
        _ada_kernel,
        grid=(n // D_MODEL,),
        in_specs=[pl.BlockSpec((8, D_MODEL), lambda j: (0, 0)),
                  pl.BlockSpec((D_MODEL, D_MODEL), lambda j: (0, j)),
                  pl.BlockSpec((1, D_MODEL), lambda j: (0, j))],
        out_specs=pl.BlockSpec((8, D_MODEL), lambda j: (0, j)),
        out_shape=jax.ShapeDtypeStruct((8, n), F32),
        name="ada",
    )(c_pad, w_ada, b_ada)


def _inproj_kernel(x_ref, ada_ref, nw_ref, w_ref, pos_ref, invf_ref, lb_ref, rnw_ref,
                   q_ref, k_ref, v_ref, rq_ref, g_ref, ri_ref, gate_ref, ga_ref, gr_ref):
    ada = ada_ref[0]
    h = (_rms(x_ref[...]) * nw_ref[...] * (1.0 + ada[1:2]) + ada[0:1]).astype(BF16)

    ang = pos_ref[...].astype(F32) * invf_ref[...]
    lane = lax.broadcasted_iota(jnp.int32, (1, HEAD_W), 1) % QK_DIM
    cos_a = jnp.cos(ang)
    sin_a = jnp.sin(ang)
    cs = jnp.where(lane < ROPE_DIM, cos_a, 1.0)
    s1 = jnp.where(lane < ROPE_DIM // 2, -sin_a, 0.0)
    s2 = jnp.where((lane >= ROPE_DIM // 2) & (lane < ROPE_DIM), sin_a, 0.0)

    def section(j):
        return jnp.dot(h, w_ref[:, j * D_MODEL:(j + 1) * D_MODEL], preferred_element_type=F32)

    def rope_store(ref, acc, scale):
        c_, s1_, s2_ = cs * scale, s1 * scale, s2 * scale
        for hh in range(HEADS):
            t = acc[:, hh * HEAD_W:(hh + 1) * HEAD_W]
            r = (t * c_ + pltpu.roll(t, HEAD_W - ROPE_DIM // 2, 1) * s1_
                 + pltpu.roll(t, ROPE_DIM // 2, 1) * s2_)
            ref[:, hh * HEAD_W:(hh + 1) * HEAD_W] = r.astype(ref.dtype)

    rope_store(q_ref, section(0), QK_DIM ** -0.5 * LOG2_E)
    rope_store(k_ref, section(1), 1.0)
    v_ref[...] = section(2).astype(v_ref.dtype)
    rq_ref[...] = section(3).astype(rq_ref.dtype)

    l0 = lb_ref[0:1, :]
    l1 = lb_ref[1:2, :]
    mx = jnp.maximum(l0, l1)
    e0 = jnp.exp(l0 - mx)
    e1 = jnp.exp(l1 - mx)
    lb = e1 / (e0 + e1)
    g_ref[...] = jnp.log(lb + (1.0 - lb) * jax.nn.sigmoid(section(4))) * LOG2_E

    ri_ref[...] = section(5).astype(ri_ref.dtype)
    rg = section(6)
    gate_ref[...] = (rnw_ref[...] * (rg * jax.nn.sigmoid(rg))).astype(gate_ref.dtype)
    ga_ref[...] = jax.nn.sigmoid(section(7)).astype(ga_ref.dtype)
    gr_ref[...] = jax.nn.sigmoid(section(8)).astype(gr_ref.dtype)


def _inproj(x2d, ada, norm_w, w_in, pos, invf, lb_logits, rec_norm_row, *, seq, tm):
    t = x2d.shape[0]
    tiles_per_batch = seq // tm
    row = lambda i: (i, 0)
    const2 = lambda i: (0, 0)
    out_dtypes = (BF16, BF16, BF16, BF16, F32, BF16, BF16, BF16, BF16)
    return pl.pallas_call(
        _inproj_kernel,
        grid=(t // tm,),
        in_specs=[pl.BlockSpec((tm, D_MODEL), row),
                  pl.BlockSpec((1, 8, D_MODEL), lambda i: (i // tiles_per_batch, 0, 0)),
                  pl.BlockSpec((1, D_MODEL), const2),
                  pl.BlockSpec((D_MODEL, N_SECTIONS * D_MODEL), const2, pipeline_mode=pl.Buffered(1)),
                  pl.BlockSpec((tm, 1), row),
                  pl.BlockSpec((1, HEAD_W), const2),
                  pl.BlockSpec((2, D_MODEL), const2),
                  pl.BlockSpec((1, D_MODEL), const2)],
        out_specs=[pl.BlockSpec((tm, D_MODEL), row) for _ in out_dtypes],
        out_shape=[jax.ShapeDtypeStruct((t, D_MODEL), dt) for dt in out_dtypes],
        compiler_params=pltpu.CompilerParams(
            dimension_semantics=("arbitrary",), vmem_limit_bytes=VMEM_LIMIT),
        name="inproj",
    )(x2d, ada, norm_w, w_in, pos, invf, lb_logits, rec_norm_row)


def _attn_kernel(lq1_ref, lk1_ref, lq2_ref, lk2_ref, sw_ref, q_ref, k_ref, v_ref, o_ref,
                 vt_scr, acc, m, l, dev, *, tq):
    qi = pl.program_id(2)
    n_tiles = pl.num_programs(2)
    n_streams = 2 * HEADS_PER_STEP

    @pl.when(qi == 0)
    def _():
        def tr(j, carry):
            start = pl.multiple_of(j * tq, tq)
            vt_scr[:, pl.ds(start, tq)] = v_ref[0, pl.ds(start, tq), :].T
            return carry
        lax.fori_loop(0, n_tiles, tr, 0)

    def head_lanes(s):
        return slice((s // 2) * HEAD_W, (s // 2 + 1) * HEAD_W)

    q = q_ref[0]
    lane = lax.broadcasted_iota(jnp.int32, (1, HEAD_W), 1)
    q_streams = []
    for s in range(n_streams):
        qh = q[:, head_lanes(s)]
        keep = (lane < QK_DIM) if s % 2 == 0 else (lane >= QK_DIM)
        q_streams.append(jnp.where(keep, qh, jnp.zeros_like(qh)))

    def init():
        acc[...] = jnp.zeros_like(acc)
        m[...] = jnp.full_like(m, -jnp.inf)
        l[...] = jnp.zeros_like(l)
        dev[...] = jnp.zeros_like(dev)

    def tiles(kj):
        start = pl.multiple_of(kj * tq, tq)
        return k_ref[0, pl.ds(start, tq), :], vt_scr[:, pl.ds(start, tq)]

    def exact_step(kj):
        k, vt = tiles(kj)
        for s in range(n_streams):
            hl = head_lanes(s)
            st = lax.dot_general(k[:, hl], q_streams[s], _NT, preferred_element_type=F32)
            kr = lax.broadcasted_iota(jnp.int32, st.shape, 0) + (kj - qi) * tq
            qc = lax.broadcasted_iota(jnp.int32, st.shape, 1)
            st = jnp.where(kr <= qc, st, -jnp.inf)
            m_prev = m[s]
            m_new = jnp.maximum(m_prev, jnp.max(st, axis=0, keepdims=True))
            alpha = jnp.exp2(m_prev - m_new)
            pt = jnp.exp2(st - m_new)
            l[s] = alpha * l[s] + jnp.sum(pt, axis=0, keepdims=True)
            acc[s] = alpha * acc[s] + jnp.dot(vt[hl, :], pt.astype(BF16), preferred_element_type=F32)
            m[s] = m_new

    def diag_step():
        k, vt = tiles(qi)
        half = tq // 2
        problems = ((slice(0, half), slice(0, half)), (slice(0, tq), slice(half, tq)))
        sts = {(s, pi): lax.dot_general(k[ks, head_lanes(s)], q_streams[s][qs], _NT,
                                        preferred_element_type=F32)
               for s in range(n_streams) for pi, (ks, qs) in enumerate(problems)}
        dev[...] = jnp.zeros_like(dev)
        for s in range(n_streams):
            for pi, (ks, qs) in enumerate(problems):
                st = sts[s, pi]
                kr = lax.broadcasted_iota(jnp.int32, st.shape, 0)
                qc = lax.broadcasted_iota(jnp.int32, st.shape, 1) + qs.start
                st = jnp.where(kr <= qc, st, -jnp.inf)
                m_new = jnp.max(st, axis=0, keepdims=True)
                pt = jnp.exp2(st - m_new)
                l[s, :, qs] = jnp.sum(pt, axis=0, keepdims=True)
                acc[s, :, qs] = jnp.dot(vt[head_lanes(s), ks], pt.astype(BF16),
                                        preferred_element_type=F32)
                m[s, :, qs] = m_new

    def stream_tiles(kj0, n):
        kv = [tiles(kj0 + t) for t in range(n)]
        n_sub = tq // SUB_KEYS
        chains = [(t, s, h) for t in range(n) for s in range(n_streams) for h in range(n_sub)]

        def qk(c):
            t, s, h = c
            ks = slice(h * SUB_KEYS, (h + 1) * SUB_KEYS)
            return lax.dot_general(kv[t][0][ks, head_lanes(s)], q_streams[s], _NT,
                                   preferred_element_type=F32)

        scores = {i: qk(c) for i, c in enumerate(chains[:QK_AHEAD])}
        part = {}
        for i, (t, s, h) in enumerate(chains):
            st = scores.pop(i)
            m_prev = m[s]
            pt = jnp.exp2(st - m_prev)
            mx = jnp.max(st, axis=0, keepdims=True)
            sm = jnp.sum(pt, axis=0, keepdims=True)
            if i + QK_AHEAD < len(chains):
                scores[i + QK_AHEAD] = qk(chains[i + QK_AHEAD])
            ks = slice(h * SUB_KEYS, (h + 1) * SUB_KEYS)
            pp = jnp.dot(kv[t][1][head_lanes(s), ks], pt.astype(BF16), preferred_element_type=F32)
            if h == 0:
                part[s] = (mx, sm, pp)
            else:
                mx0, sm0, pp0 = part[s]
                part[s] = (jnp.maximum(mx0, mx), sm0 + sm, pp0 + pp)
            if h == n_sub - 1:
                t_max, l_add, pv = part.pop(s)
                m_new = jnp.maximum(m_prev, t_max)
                alpha = jnp.exp2(m_prev - m_new)
                dev[s] = jnp.maximum(dev[s], t_max - m_prev)
                l[s] = alpha * (l[s] + l_add)
                acc[s] = alpha * (acc[s] + pv)
                m[s] = m_new

    diag_step()

    def body(t, carry):
        stream_tiles(STREAM_TILES * t, STREAM_TILES)
        return carry

    lax.fori_loop(0, qi // STREAM_TILES, body, 0)

    for r in range(1, STREAM_TILES):
        @pl.when(qi % STREAM_TILES == r)
        def _(r=r):
            stream_tiles(qi - r, r)

    rise = jnp.max(dev[...])

    @pl.when(jnp.logical_not(rise <= MAX_RISE))
    def _():
        init()

        def redo(kj, carry):
            exact_step(kj)
            return carry
        lax.fori_loop(0, qi + 1, redo, 0)

    lam = (jnp.exp(jnp.sum(lq1_ref[...] * lk1_ref[...], axis=-1, keepdims=True))
           - jnp.exp(jnp.sum(lq2_ref[...] * lk2_ref[...], axis=-1, keepdims=True)) + LAM_INIT)
    for hh in range(HEADS_PER_STEP):
        s1, s2 = 2 * hh, 2 * hh + 1
        ot = acc[s1] / l[s1] - lam * (acc[s2] / l[s2])
        inv = lax.rsqrt(jnp.mean(ot * ot, axis=0, keepdims=True) + EPS)
        ot = ot * inv * sw_ref[...] * (1.0 - LAM_INIT)
        o_ref[0, :, hh * HEAD_W:(hh + 1) * HEAD_W] = ot.T.astype(o_ref.dtype)


def _attn(q3, k3, v3, lq1, lk1, lq2, lk2, subln_col, *, tq):
    b, s, _ = q3.shape
    vec = lambda bi, h, qi: (0, 0)
    width = HEADS_PER_STEP * HEAD_W
    n_streams = 2 * HEADS_PER_STEP
    kernel = functools.partial(_attn_kernel, tq=tq)
    return pl.pallas_call(
        kernel,
        grid=(b, HEADS // HEADS_PER_STEP, s // tq),
        in_specs=[pl.BlockSpec((1, QK_DIM), vec)] * 4
                 + [pl.BlockSpec((HEAD_W, 1), vec),
                    pl.BlockSpec((1, tq, width), lambda bi, h, qi: (bi, qi, h)),
                    pl.BlockSpec((1, s, width), lambda bi, h, qi: (bi, 0, h)),
                    pl.BlockSpec((1, s, width), lambda bi, h, qi: (bi, 0, h))],
        out_specs=pl.BlockSpec((1, tq, width), lambda bi, h, qi: (bi, qi, h)),
        out_shape=jax.ShapeDtypeStruct((b, s, D_MODEL), BF16),
        scratch_shapes=[pltpu.VMEM((width, s), BF16),
                        pltpu.VMEM((n_streams, HEAD_W, tq), F32),
                        pltpu.VMEM((n_streams, 1, tq), F32),
                        pltpu.VMEM((n_streams, 1, tq), F32),
                        pltpu.VMEM((n_streams, 1, tq), F32)],
        compiler_params=pltpu.CompilerParams(
            dimension_semantics=("arbitrary", "arbitrary", "arbitrary"), vmem_limit_bytes=VMEM_LIMIT),
        name="attn",
    )(lq1, lk1, lq2, lk2, subln_col, q3, k3, v3)


def _hgrn_kernel(rq_ref, g_ref, ri_ref, gate_ref, o_ref, st_scr, sc_scr, *, n_chunks):
    c_len = REC_CHUNK

    @pl.when(pl.program_id(1) == 0)
    def _():
        st_scr[...] = jnp.zeros_like(st_scr)

    rows_i = lax.broadcasted_iota(jnp.int32, (c_len, c_len), 0)
    cols_i = lax.broadcasted_iota(jnp.int32, (c_len, c_len), 1)
    causal = rows_i >= cols_i
    row_col = lax.broadcasted_iota(jnp.int32, (c_len, 1), 0)

    n_rows = n_chunks * c_len
    t_row = lax.broadcasted_iota(jnp.int32, (n_rows, n_rows), 0)
    t_col = lax.broadcasted_iota(jnp.int32, (n_rows, n_rows), 1)
    tri = ((t_row // c_len == t_col // c_len) & (t_row >= t_col)).astype(BF16)
    g_all = g_ref[...]
    g_hi = g_all.astype(BF16)
    rest = g_all - g_hi.astype(F32)
    g_mid = rest.astype(BF16)
    g_lo = (rest - g_mid.astype(F32)).astype(BF16)
    b_all = (jnp.dot(tri, g_hi, preferred_element_type=F32)
             + jnp.dot(tri, g_mid, preferred_element_type=F32)
             + jnp.dot(tri, g_lo, preferred_element_type=F32))

    def prepare(c):
        rows = slice(c * c_len, (c + 1) * c_len)
        g = g_all[rows]
        b = b_all[rows]
        b_last = b[c_len - 1:c_len, :]
        b_mid = b[c_len // 2 - 1:c_len // 2, :]
        safe = jnp.maximum(jnp.max(-b_mid), jnp.max(b_mid - b_last)) < SAFE_DECAY
        return rows, g, b, b_last, b_mid, safe

    def factored_scores(q, kk, b, b_mid):
        qb = (q * jnp.exp2(b - b_mid)).astype(BF16)
        kh = (kk * jnp.exp2(b_mid - b)).astype(BF16)

        def head(h):
            sl = slice(h * HEAD_W, (h + 1) * HEAD_W)
            s = lax.dot_general(qb[:, sl], kh[:, sl], _NT, preferred_element_type=F32)
            return jnp.where(causal, s, 0.0)
        return head

    def exact_scores(q, kk, b):
        def head(h):
            sl = slice(h * HEAD_W, (h + 1) * HEAD_W)
            qh, bh, kh = q[:, sl], b[:, sl], kk[:, sl]

            def col_body(s_idx, scores):
                pick = row_col == s_idx
                b_row = jnp.sum(jnp.where(pick, bh, 0.0), axis=0, keepdims=True)
                k_row = jnp.sum(jnp.where(pick, kh, 0.0), axis=0, keepdims=True)
                e = jnp.exp2(jnp.minimum(bh - b_row, 0.0))
                col = jnp.sum(qh * k_row * e, axis=1, keepdims=True)
                col = jnp.where(row_col >= s_idx, col, 0.0)
                return jnp.where(cols_i == s_idx, col, scores)

            return lax.fori_loop(0, c_len, col_body, jnp.zeros((c_len, c_len), F32))
        return head

    def finish(rows, q, kk, b, b_last, scores_of_head):
        qb = (q * jnp.exp2(b)).astype(BF16)
        kdb = (kk * jnp.exp2(b_last - b)).astype(BF16)
        decay_last = jnp.exp2(b_last)
        for h in range(HEADS):
            sl = slice(h * HEAD_W, (h + 1) * HEAD_W)
            v = ri_ref[rows, sl]
            st = st_scr[h]
            o = (jnp.dot(scores_of_head(h).astype(BF16), v, preferred_element_type=F32)
                 + lax.dot_general(qb[:, sl], st.astype(BF16), _NT, preferred_element_type=F32))
            st_scr[h] = st * decay_last[:, sl] + lax.dot_general(v, kdb[:, sl], _TN,
                                                                 preferred_element_type=F32)
            o_ref[rows, sl] = (_rms(o) * gate_ref[rows, sl].astype(F32)).astype(o_ref.dtype)

    prepared = [prepare(c) for c in range(n_chunks)]
    all_safe = functools.reduce(jnp.logical_and, [p[5] for p in prepared])

    @pl.when(all_safe)
    def _():
        for rows, g, b, b_last, b_mid, _ in prepared:
            q = rq_ref[rows, :].astype(F32)
            kk = 1.0 - jnp.exp2(g)
            finish(rows, q, kk, b, b_last, factored_scores(q, kk, b, b_mid))

    @pl.when(jnp.logical_not(all_safe))
    def _():
        for rows, g, b, b_last, b_mid, safe in prepared:
            q = rq_ref[rows, :].astype(F32)
            kk = 1.0 - jnp.exp2(g)

            @pl.when(safe)
            def _():
                head = factored_scores(q, kk, b, b_mid)
                for h in range(HEADS):
                    sc_scr[h] = head(h)

            @pl.when(jnp.logical_not(safe))
            def _():
                head = exact_scores(q, kk, b)
                for h in range(HEADS):
                    sc_scr[h] = head(h)

            finish(rows, q, kk, b, b_last, lambda h: sc_scr[h])


def _hgrn(rq, g, ri, gate, *, batch, seq, tt):
    t = rq.shape[0]
    tiles = seq // tt
    row = lambda bi, ti: (bi * tiles + ti, 0)
    kernel = functools.partial(_hgrn_kernel, n_chunks=tt // REC_CHUNK)
    return pl.pallas_call(
        kernel,
        grid=(batch, tiles),
        in_specs=[pl.BlockSpec((tt, D_MODEL), row)] * 4,
        out_specs=pl.BlockSpec((tt, D_MODEL), row),
        out_shape=jax.ShapeDtypeStruct((t, D_MODEL), BF16),
        scratch_shapes=[pltpu.VMEM((HEADS, HEAD_W, HEAD_W), F32),
                        pltpu.VMEM((HEADS, REC_CHUNK, REC_CHUNK), F32)],
        compiler_params=pltpu.CompilerParams(
            dimension_semantics=("arbitrary", "arbitrary"), vmem_limit_bytes=VMEM_LIMIT),
        name="hgrn",
    )(rq, g, ri, gate)


def _merge_kernel(oa_ref, or_ref, ga_ref, gr_ref, x_ref, ada_ref, wa_ref, wr_ref, wo_ref, o_ref):
    ya = jnp.dot(oa_ref[...], wa_ref[...], preferred_element_type=F32)
    yr = jnp.dot(or_ref[...], wr_ref[...], preferred_element_type=F32)
    y = ga_ref[...].astype(F32) * ya + gr_ref[...].astype(F32) * yr
    upd = jnp.dot(y.astype(BF16), wo_ref[...], preferred_element_type=F32)
    o_ref[...] = x_ref[...] + ada_ref[0][2:3] * upd


def _merge(oa, orr, ga, gr, x2d, ada, wa, wr, wo, *, seq, tm):
    t = x2d.shape[0]
    tiles_per_batch = seq // tm
    row = lambda i: (i, 0)
    wspec = pl.BlockSpec((D_MODEL, D_MODEL), lambda i: (0, 0))
    return pl.pallas_call(
        _merge_kernel,
        grid=(t // tm,),
        in_specs=[pl.BlockSpec((tm, D_MODEL), row)] * 5
                 + [pl.BlockSpec((1, 8, D_MODEL), lambda i: (i // tiles_per_batch, 0, 0)),
                    wspec, wspec, wspec],
        out_specs=pl.BlockSpec((tm, D_MODEL), row),
        out_shape=jax.ShapeDtypeStruct((t, D_MODEL), F32),
        compiler_params=pltpu.CompilerParams(
            dimension_semantics=("arbitrary",), vmem_limit_bytes=VMEM_LIMIT),
        name="merge",
    )(oa, orr, ga, gr, x2d, ada, wa, wr, wo)


def _mlp_kernel(x_ref, ada_ref, nw_ref, nf_ref, w1_ref, w2_ref, o_ref):
    x = x_ref[...]
    ada = ada_ref[0]
    h = (_rms(x) * nw_ref[...] * (1.0 + ada[4:5]) + ada[3:4]).astype(BF16)
    acc = jnp.zeros_like(x)
    for c in range(D_FF // D_MODEL):
        cols = slice(c * D_MODEL, (c + 1) * D_MODEL)
        u = jnp.dot(h, w1_ref[:, cols], preferred_element_type=F32)
        u = jnp.square(jnp.maximum(u, 0.0))
        acc = acc + jnp.dot(u.astype(BF16), w2_ref[cols, :], preferred_element_type=F32)
    o_ref[...] = _rms(x + ada[5:6] * acc) * nf_ref[...]


def _mlp(x1, ada, norm_w, norm_final, w1, w2, *, seq, tm):
    t = x1.shape[0]
    tiles_per_batch = seq // tm
    row = lambda i: (i, 0)
    const = lambda i: (0, 0)
    return pl.pallas_call(
        _mlp_kernel,
        grid=(t // tm,),
        in_specs=[pl.BlockSpec((tm, D_MODEL), row),
                  pl.BlockSpec((1, 8, D_MODEL), lambda i: (i // tiles_per_batch, 0, 0)),
                  pl.BlockSpec((1, D_MODEL), const),
                  pl.BlockSpec((1, D_MODEL), const),
                  pl.BlockSpec((D_MODEL, D_FF), const, pipeline_mode=pl.Buffered(1)),
                  pl.BlockSpec((D_FF, D_MODEL), const, pipeline_mode=pl.Buffered(1))],
        out_specs=pl.BlockSpec((tm, D_MODEL), row),
        out_shape=jax.ShapeDtypeStruct((t, D_MODEL), F32),
        compiler_params=pltpu.CompilerParams(
            dimension_semantics=("arbitrary",), vmem_limit_bytes=VMEM_LIMIT),
        name="mlp",
    )(x1, ada, norm_w, norm_final, w1, w2)


def _rope_inv_freq_lanes():
    inv_freq = ROPE_THETA ** (-jnp.arange(0, ROPE_DIM, 2, dtype=F32) / ROPE_DIM)
    lane = jnp.arange(HEAD_W) % QK_DIM
    table = jnp.where(lane < ROPE_DIM, inv_freq[lane % (ROPE_DIM // 2)], 0.0)
    return table.reshape(1, HEAD_W).astype(F32)


def kernel(x, c, positions, w_ada, b_ada, norm_mix, w_in, lam_q1, lam_k1, lam_q2, lam_k2, subln_w, lb_logits, rec_norm_w, w_proj_att, w_proj_rec, w_out, norm_mlp, w_mlp_in, w_mlp_out, norm_final):
    batch, seq, d = x.shape
    assert d == D_MODEL and w_ada.shape[0] == 1, "single-layer, d_model=1024 only"
    assert batch <= 8
    t = batch * seq
    tm = min(512, seq)
    tq = min(512, seq)
    tt = min(256, seq)
    assert seq % tm == 0 and seq % tq == 0 and seq % tt == 0 and tt % REC_CHUNK == 0

    x2d = x.reshape(t, D_MODEL)
    c_pad = jnp.zeros((8, D_MODEL), F32).at[:batch].set(c)
    ada = _ada(c_pad, w_ada[0], b_ada[0].reshape(1, -1))[:batch]
    ada = jnp.pad(ada.reshape(batch, N_ADA, D_MODEL), ((0, 0), (0, 8 - N_ADA), (0, 0)))

    q, k, v, rq, g, ri, gate, ga, gr = _inproj(
        x2d, ada, norm_mix[0].reshape(1, -1), w_in[0].astype(BF16),
        positions.reshape(t, 1), _rope_inv_freq_lanes(), lb_logits,
        jnp.tile(rec_norm_w.reshape(1, HEAD_W), (1, HEADS)), seq=seq, tm=tm)

    as3 = lambda a: a.reshape(batch, seq, D_MODEL)
    o_a = _attn(as3(q), as3(k), as3(v), lam_q1, lam_k1, lam_q2, lam_k2,
                subln_w.reshape(HEAD_W, 1), tq=tq)
    o_r = _hgrn(rq, g, ri, gate, batch=batch, seq=seq, tt=tt)

    x1 = _merge(o_a.reshape(t, D_MODEL), o_r, ga, gr, x2d, ada,
                w_proj_att[0].astype(BF16), w_proj_rec[0].astype(BF16), w_out[0].astype(BF16),
                seq=seq, tm=tm)
    out = _mlp(x1, ada, norm_mlp[0].reshape(1, -1), norm_final.reshape(1, -1),
               w_mlp_in[0].astype(BF16), w_mlp_out[0].astype(BF16), seq=seq, tm=tm)
    return out.reshape(batch, seq, D_MODEL)
```

```python
import functools
import math

import jax
import jax.numpy as jnp
from jax import lax
from jax.experimental import pallas as pl
from jax.experimental.pallas import tpu as pltpu

F32 = jnp.float32
BF16 = jnp.bfloat16

D_MODEL = 1024
HEADS = 8
HEAD_W = 128
QK_DIM = 64
ROPE_DIM = 16
ROPE_THETA = 500000.0
D_FF = 4 * D_MODEL
N_ADA = 6
N_SECTIONS = 9
REC_CHUNK = 64
EPS = 1e-6
LAM_INIT = 0.8 - 0.6 * math.exp(-0.3 * 0)
LOG2_E = 1.0 / math.log(2.0)
SAFE_DECAY = 115.0
MAX_RISE = 64.0
SUB_KEYS = 256
STREAM_TILES = 4
QK_AHEAD = 2
HEADS_PER_STEP = 2
VMEM_LIMIT = 56 * 1024 * 1024

_NT = (((1,), (1,)), ((), ()))
_TN = (((0,), (0,)), ((), ()))


def _rms(x):
    return x * lax.rsqrt(jnp.mean(x * x, axis=-1, keepdims=True) + EPS)


def _ada_kernel(c_ref, w_ref, b_ref, o_ref):
    c = c_ref[...]
    cond = c * jax.nn.sigmoid(c)
    o_ref[...] = jnp.dot(cond, w_ref[...], preferred_element_type=F32,
                         precision=lax.Precision.HIGHEST) + b_ref[...]


def _ada(c_pad, w_ada, b_ada):
    n = w_ada.shape[1]
    return pl.pallas_call(
        _ada_kernel,
        grid=(n // D_MODEL,),
        in_specs=[pl.BlockSpec((8, D_MODEL), lambda j: (0, 0)),
                  pl.BlockSpec((D_MODEL, D_MODEL), lambda j: (0, j)),
                  pl.BlockSpec((1, D_MODEL), lambda j: (0, j))],
        out_specs=pl.BlockSpec((8, D_MODEL), lambda j: (0, j)),
        out_shape=jax.ShapeDtypeStruct((8, n), F32),
        name="ada",
    )(c_pad, w_ada, b_ada)


def _inproj_kernel(x_ref, ada_ref, nw_ref, w_ref, pos_ref, invf_ref, lb_ref, rnw_ref,
                   q_ref, k_ref, v_ref, rq_ref, g_ref, ri_ref, gate_ref, ga_ref, gr_ref):
    ada = ada_ref[0]
    h = (_rms(x_ref[...]) * nw_ref[...] * (1.0 + ada[1:2]) + ada[0:1]).astype(BF16)

    ang = pos_ref[...].astype(F32) * invf_ref[...]
    lane = lax.broadcasted_iota(jnp.int32, (1, HEAD_W), 1) % QK_DIM
    cos_a = jnp.cos(ang)
    sin_a = jnp.sin(ang)
    cs = jnp.where(lane < ROPE_DIM, cos_a, 1.0)
    s1 = jnp.where(lane < ROPE_DIM // 2, -sin_a, 0.0)
    s2 = jnp.where((lane >= ROPE_DIM // 2) & (lane < ROPE_DIM), sin_a, 0.0)

    def section(j):
        return jnp.dot(h, w_ref[:, j * D_MODEL:(j + 1) * D_MODEL], preferred_element_type=F32)

    def rope_store(ref, acc, scale):
        c_, s1_, s2_ = cs * scale, s1 * scale, s2 * scale
        for hh in range(HEADS):
            t = acc[:, hh * HEAD_W:(hh + 1) * HEAD_W]
            r = (t * c_ + pltpu.roll(t, HEAD_W - ROPE_DIM // 2, 1) * s1_
                 + pltpu.roll(t, ROPE_DIM // 2, 1) * s2_)
            ref[:, hh * HEAD_W:(hh + 1) * HEAD_W] = r.astype(ref.dtype)

    v_ref[...] = section(2).astype(v_ref.dtype)
    rope_store(q_ref, section(0), QK_DIM ** -0.5 * LOG2_E)
    rq_ref[...] = section(3).astype(rq_ref.dtype)
    rope_store(k_ref, section(1), 1.0)
    ri_ref[...] = section(5).astype(ri_ref.dtype)

    l0 = lb_ref[0:1, :]
    l1 = lb_ref[1:2, :]
    mx = jnp.maximum(l0, l1)
    e0 = jnp.exp(l0 - mx)
    e1 = jnp.exp(l1 - mx)
    lb = e1 / (e0 + e1)
    g_ref[...] = jnp.log(lb + (1.0 - lb) * jax.nn.sigmoid(section(4))) * LOG2_E

    rg = section(6)
    gate_ref[...] = (rnw_ref[...] * (rg * jax.nn.sigmoid(rg))).astype(gate_ref.dtype)
    ga_ref[...] = jax.nn.sigmoid(section(7)).astype(ga_ref.dtype)
    gr_ref[...] = jax.nn.sigmoid(section(8)).astype(gr_ref.dtype)


def _inproj(x2d, ada, norm_w, w_in, pos, invf, lb_logits, rec_norm_row, *, seq, tm):
    t = x2d.shape[0]
    tiles_per_batch = seq // tm
    row = lambda i: (i, 0)
    const2 = lambda i: (0, 0)
    out_dtypes = (BF16, BF16, BF16, BF16, F32, BF16, BF16, BF16, BF16)
    return pl.pallas_call(
        _inproj_kernel,
        grid=(t // tm,),
        in_specs=[pl.BlockSpec((tm, D_MODEL), row),
                  pl.BlockSpec((1, 8, D_MODEL), lambda i: (i // tiles_per_batch, 0, 0)),
                  pl.BlockSpec((1, D_MODEL), const2),
                  pl.BlockSpec((D_MODEL, N_SECTIONS * D_MODEL), const2, pipeline_mode=pl.Buffered(1)),
                  pl.BlockSpec((tm, 1), row),
                  pl.BlockSpec((1, HEAD_W), const2),
                  pl.BlockSpec((2, D_MODEL), const2),
                  pl.BlockSpec((1, D_MODEL), const2)],
        out_specs=[pl.BlockSpec((tm, D_MODEL), row) for _ in out_dtypes],
        out_shape=[jax.ShapeDtypeStruct((t, D_MODEL), dt) for dt in out_dtypes],
        compiler_params=pltpu.CompilerParams(
            dimension_semantics=("arbitrary",), vmem_limit_bytes=VMEM_LIMIT),
        name="inproj",
    )(x2d, ada, norm_w, w_in, pos, invf, lb_logits, rec_norm_row)


def _attn_kernel(lq1_ref, lk1_ref, lq2_ref, lk2_ref, sw_ref, q_ref, k_ref, v_ref, o_ref,
                 vt_scr, acc, m, l, dev, *, tq):
    qi = pl.program_id(2)
    n_tiles = pl.num_programs(2)
    n_streams = 2 * HEADS_PER_STEP

    @pl.when(qi == 0)
    def _():
        def tr(j, carry):
            start = pl.multiple_of(j * tq, tq)
            vt_scr[:, pl.ds(start, tq)] = v_ref[0, pl.ds(start, tq), :].T
            return carry
        lax.fori_loop(0, n_tiles, tr, 0)

    def head_lanes(s):
        return slice((s // 2) * HEAD_W, (s // 2 + 1) * HEAD_W)

    q = q_ref[0]
    lane = lax.broadcasted_iota(jnp.int32, (1, HEAD_W), 1)
    q_streams = []
    for s in range(n_streams):
        qh = q[:, head_lanes(s)]
        keep = (lane < QK_DIM) if s % 2 == 0 else (lane >= QK_DIM)
        q_streams.append(jnp.where(keep, qh, jnp.zeros_like(qh)))

    def init():
        acc[...] = jnp.zeros_like(acc)
        m[...] = jnp.full_like(m, -jnp.inf)
        l[...] = jnp.zeros_like(l)
        dev[...] = jnp.zeros_like(dev)

    def tiles(kj):
        start = pl.multiple_of(kj * tq, tq)
        return k_ref[0, pl.ds(start, tq), :], vt_scr[:, pl.ds(start, tq)]

    def exact_step(kj):
        k, vt = tiles(kj)
        for s in range(n_streams):
            hl = head_lanes(s)
            st = lax.dot_general(k[:, hl], q_streams[s], _NT, preferred_element_type=F32)
            kr = lax.broadcasted_iota(jnp.int32, st.shape, 0) + (kj - qi) * tq
            qc = lax.broadcasted_iota(jnp.int32, st.shape, 1)
            st = jnp.where(kr <= qc, st, -jnp.inf)
            m_prev = m[s]
            m_new = jnp.maximum(m_prev, jnp.max(st, axis=0, keepdims=True))
            alpha = jnp.exp2(m_prev - m_new)
            pt = jnp.exp2(st - m_new)
            l[s] = alpha * l[s] + jnp.sum(pt, axis=0, keepdims=True)
            acc[s] = alpha * acc[s] + jnp.dot(vt[hl, :], pt.astype(BF16), preferred_element_type=F32)
            m[s] = m_new

    def diag_step():
        k, vt = tiles(qi)
        half = tq // 2
        problems = ((slice(0, half), slice(0, half)), (slice(0, tq), slice(half, tq)))
        sts = {(s, pi): lax.dot_general(k[ks, head_lanes(s)], q_streams[s][qs], _NT,
                                        preferred_element_type=F32)
               for s in range(n_streams) for pi, (ks, qs) in enumerate(problems)}
        dev[...] = jnp.zeros_like(dev)
        for s in range(n_streams):
            for pi, (ks, qs) in enumerate(problems):
                st = sts[s, pi]
                kr = lax.broadcasted_iota(jnp.int32, st.shape, 0)
                qc = lax.broadcasted_iota(jnp.int32, st.shape, 1) + qs.start
                st = jnp.where(kr <= qc, st, -jnp.inf)
                m_new = jnp.max(st, axis=0, keepdims=True)
                pt = jnp.exp2(st - m_new)
                l[s, :, qs] = jnp.sum(pt, axis=0, keepdims=True)
                acc[s, :, qs] = jnp.dot(vt[head_lanes(s), ks], pt.astype(BF16),
                                        preferred_element_type=F32)
                m[s, :, qs] = m_new

    def stream_tiles(kj0, n):
        kv = [tiles(kj0 + t) for t in range(n)]
        n_sub = tq // SUB_KEYS
        chains = [(t, s, h) for t in range(n) for s in range(n_streams) for h in range(n_sub)]

        def qk(c):
            t, s, h = c
            ks = slice(h * SUB_KEYS, (h + 1) * SUB_KEYS)
            return lax.dot_general(kv[t][0][ks, head_lanes(s)], q_streams[s], _NT,
                                   preferred_element_type=F32)

        scores = {i: qk(c) for i, c in enumerate(chains[:QK_AHEAD])}
        part = {}
        for i, (t, s, h) in enumerate(chains):
            st = scores.pop(i)
            m_prev = m[s]
            pt = jnp.exp2(st - m_prev)
            mx = jnp.max(st, axis=0, keepdims=True)
            sm = jnp.sum(pt, axis=0, keepdims=True)
            if i + QK_AHEAD < len(chains):
                scores[i + QK_AHEAD] = qk(chains[i + QK_AHEAD])
            ks = slice(h * SUB_KEYS, (h + 1) * SUB_KEYS)
            pp = jnp.dot(kv[t][1][head_lanes(s), ks], pt.astype(BF16), preferred_element_type=F32)
            if h == 0:
                part[s] = (mx, sm, pp)
            else:
                mx0, sm0, pp0 = part[s]
                part[s] = (jnp.maximum(mx0, mx), sm0 + sm, pp0 + pp)
            if h == n_sub - 1:
                t_max, l_add, pv = part.pop(s)
                m_new = jnp.maximum(m_prev, t_max)
                alpha = jnp.exp2(m_prev - m_new)
                dev[s] = jnp.maximum(dev[s], t_max - m_prev)
                l[s] = alpha * (l[s] + l_add)
                acc[s] = alpha * (acc[s] + pv)
                m[s] = m_new

    diag_step()

    def body(t, carry):
        stream_tiles(STREAM_TILES * t, STREAM_TILES)
        return carry

    lax.fori_loop(0, qi // STREAM_TILES, body, 0)

    for r in range(1, STREAM_TILES):
        @pl.when(qi % STREAM_TILES == r)
        def _(r=r):
            stream_tiles(qi - r, r)

    lam = (jnp.exp(jnp.sum(lq1_ref[...] * lk1_ref[...], axis=-1, keepdims=True))
           - jnp.exp(jnp.sum(lq2_ref[...] * lk2_ref[...], axis=-1, keepdims=True)) + LAM_INIT)

    def write_output():
        for hh in range(HEADS_PER_STEP):
            s1, s2 = 2 * hh, 2 * hh + 1
            ot = acc[s1] / l[s1] - lam * (acc[s2] / l[s2])
            inv = lax.rsqrt(jnp.mean(ot * ot, axis=0, keepdims=True) + EPS)
            ot = ot * inv * sw_ref[...] * (1.0 - LAM_INIT)
            o_ref[0, :, hh * HEAD_W:(hh + 1) * HEAD_W] = ot.T.astype(o_ref.dtype)

    write_output()

    rise = jnp.max(dev[...])

    @pl.when(jnp.logical_not(rise <= MAX_RISE))
    def _():
        init()

        def redo(kj, carry):
            exact_step(kj)
            return carry
        lax.fori_loop(0, qi + 1, redo, 0)
        write_output()


def _attn(q3, k3, v3, lq1, lk1, lq2, lk2, subln_col, *, tq):
    b, s, _ = q3.shape
    vec = lambda bi, h, qi: (0, 0)
    width = HEADS_PER_STEP * HEAD_W
    n_streams = 2 * HEADS_PER_STEP
    kernel = functools.partial(_attn_kernel, tq=tq)
    return pl.pallas_call(
        kernel,
        grid=(b, HEADS // HEADS_PER_STEP, s // tq),
        in_specs=[pl.BlockSpec((1, QK_DIM), vec)] * 4
                 + [pl.BlockSpec((HEAD_W, 1), vec),
                    pl.BlockSpec((1, tq, width), lambda bi, h, qi: (bi, qi, h)),
                    pl.BlockSpec((1, s, width), lambda bi, h, qi: (bi, 0, h)),
                    pl.BlockSpec((1, s, width), lambda bi, h, qi: (bi, 0, h))],
        out_specs=pl.BlockSpec((1, tq, width), lambda bi, h, qi: (bi, qi, h)),
        out_shape=jax.ShapeDtypeStruct((b, s, D_MODEL), BF16),
        scratch_shapes=[pltpu.VMEM((width, s), BF16),
                        pltpu.VMEM((n_streams, HEAD_W, tq), F32),
                        pltpu.VMEM((n_streams, 1, tq), F32),
                        pltpu.VMEM((n_streams, 1, tq), F32),
                        pltpu.VMEM((n_streams, 1, tq), F32)],
        compiler_params=pltpu.CompilerParams(
            dimension_semantics=("arbitrary", "arbitrary", "arbitrary"), vmem_limit_bytes=VMEM_LIMIT),
        name="attn",
    )(lq1, lk1, lq2, lk2, subln_col, q3, k3, v3)


def _hgrn_kernel(rq_ref, g_ref, ri_ref, gate_ref, o_ref, st_scr, sc_scr, *, n_chunks):
    c_len = REC_CHUNK

    @pl.when(pl.program_id(1) == 0)
    def _():
        st_scr[...] = jnp.zeros_like(st_scr)

    rows_i = lax.broadcasted_iota(jnp.int32, (c_len, c_len), 0)
    cols_i = lax.broadcasted_iota(jnp.int32, (c_len, c_len), 1)
    causal = rows_i >= cols_i
    row_col = lax.broadcasted_iota(jnp.int32, (c_len, 1), 0)

    n_rows = n_chunks * c_len
    t_row = lax.broadcasted_iota(jnp.int32, (n_rows, n_rows), 0)
    t_col = lax.broadcasted_iota(jnp.int32, (n_rows, n_rows), 1)
    tri = ((t_row // c_len == t_col // c_len) & (t_row >= t_col)).astype(BF16)
    g_all = g_ref[...]
    g_hi = g_all.astype(BF16)
    rest = g_all - g_hi.astype(F32)
    g_mid = rest.astype(BF16)
    g_lo = (rest - g_mid.astype(F32)).astype(BF16)
    b_all = (jnp.dot(tri, g_hi, preferred_element_type=F32)
             + jnp.dot(tri, g_mid, preferred_element_type=F32)
             + jnp.dot(tri, g_lo, preferred_element_type=F32))

    def prepare(c):
        rows = slice(c * c_len, (c + 1) * c_len)
        g = g_all[rows]
        b = b_all[rows]
        b_last = b[c_len - 1:c_len, :]
        b_mid = b[c_len // 2 - 1:c_len // 2, :]
        safe = jnp.maximum(jnp.max(-b_mid), jnp.max(b_mid - b_last)) < SAFE_DECAY
        return rows, g, b, b_last, b_mid, safe

    def factored_scores(q, kk, b, b_mid):
        qb = (q * jnp.exp2(b - b_mid)).astype(BF16)
        kh = (kk * jnp.exp2(b_mid - b)).astype(BF16)

        def head(h):
            sl = slice(h * HEAD_W, (h + 1) * HEAD_W)
            s = lax.dot_general(qb[:, sl], kh[:, sl], _NT, preferred_element_type=F32)
            return jnp.where(causal, s, 0.0)
        return head

    def exact_scores(q, kk, b):
        def head(h):
            sl = slice(h * HEAD_W, (h + 1) * HEAD_W)
            qh, bh, kh = q[:, sl], b[:, sl], kk[:, sl]

            def col_body(s_idx, scores):
                pick = row_col == s_idx
                b_row = jnp.sum(jnp.where(pick, bh, 0.0), axis=0, keepdims=True)
                k_row = jnp.sum(jnp.where(pick, kh, 0.0), axis=0, keepdims=True)
                e = jnp.exp2(jnp.minimum(bh - b_row, 0.0))
                col = jnp.sum(qh * k_row * e, axis=1, keepdims=True)
                col = jnp.where(row_col >= s_idx, col, 0.0)
                return jnp.where(cols_i == s_idx, col, scores)

            return lax.fori_loop(0, c_len, col_body, jnp.zeros((c_len, c_len), F32))
        return head

    def finish(rows, q, kk, b, b_last, scores_of_head, state):
        qb = (q * jnp.exp2(b)).astype(BF16)
        kdb = (kk * jnp.exp2(b_last - b)).astype(BF16)
        decay_last = jnp.exp2(b_last)
        lanes = [slice(h * HEAD_W, (h + 1) * HEAD_W) for h in range(HEADS)]
        vs = [ri_ref[rows, sl] for sl in lanes]
        sc = [scores_of_head(h).astype(BF16) for h in range(HEADS)]
        upd = [lax.dot_general(vs[h], kdb[:, lanes[h]], _TN, preferred_element_type=F32)
               for h in range(HEADS)]
        intra = [jnp.dot(sc[h], vs[h], preferred_element_type=F32) for h in range(HEADS)]
        inter = [lax.dot_general(qb[:, lanes[h]], state[h].astype(BF16), _NT,
                                 preferred_element_type=F32) for h in range(HEADS)]
        for h, sl in enumerate(lanes):
            state[h] = state[h] * decay_last[:, sl] + upd[h]
            o = intra[h] + inter[h]
            o_ref[rows, sl] = (_rms(o) * gate_ref[rows, sl].astype(F32)).astype(o_ref.dtype)

    prepared = [prepare(c) for c in range(n_chunks)]
    all_safe = functools.reduce(jnp.logical_and, [p[5] for p in prepared])

    state = [st_scr[h] for h in range(HEADS)]
    for rows, g, b, b_last, b_mid, _ in prepared:
        q = rq_ref[rows, :].astype(F32)
        kk = 1.0 - jnp.exp2(g)
        finish(rows, q, kk, b, b_last, factored_scores(q, kk, b, b_mid), state)

    @pl.when(all_safe)
    def _():
        for h in range(HEADS):
            st_scr[h] = state[h]

    @pl.when(jnp.logical_not(all_safe))
    def _():
        redo_state = [st_scr[h] for h in range(HEADS)]
        for rows, g, b, b_last, b_mid, safe in prepared:
            q = rq_ref[rows, :].astype(F32)
            kk = 1.0 - jnp.exp2(g)

            @pl.when(safe)
            def _():
                head = factored_scores(q, kk, b, b_mid)
                for h in range(HEADS):
                    sc_scr[h] = head(h)

            @pl.when(jnp.logical_not(safe))
            def _():
                head = exact_scores(q, kk, b)
                for h in range(HEADS):
                    sc_scr[h] = head(h)

            finish(rows, q, kk, b, b_last, lambda h: sc_scr[h], redo_state)
        for h in range(HEADS):
            st_scr[h] = redo_state[h]


def _hgrn(rq, g, ri, gate, *, batch, seq, tt):
    t = rq.shape[0]
    tiles = seq // tt
    row = lambda bi, ti: (bi * tiles + ti, 0)
    kernel = functools.partial(_hgrn_kernel, n_chunks=tt // REC_CHUNK)
    return pl.pallas_call(
        kernel,
        grid=(batch, tiles),
        in_specs=[pl.BlockSpec((tt, D_MODEL), row)] * 4,
        out_specs=pl.BlockSpec((tt, D_MODEL), row),
        out_shape=jax.ShapeDtypeStruct((t, D_MODEL), BF16),
        scratch_shapes=[pltpu.VMEM((HEADS, HEAD_W, HEAD_W), F32),
                        pltpu.VMEM((HEADS, REC_CHUNK, REC_CHUNK), F32)],
        compiler_params=pltpu.CompilerParams(
            dimension_semantics=("arbitrary", "arbitrary"), vmem_limit_bytes=VMEM_LIMIT),
        name="hgrn",
    )(rq, g, ri, gate)


def _merge_kernel(oa_ref, or_ref, ga_ref, gr_ref, x_ref, ada_ref, wa_ref, wr_ref, wo_ref, o_ref):
    ya = jnp.dot(oa_ref[...], wa_ref[...], preferred_element_type=F32)
    yr = jnp.dot(or_ref[...], wr_ref[...], preferred_element_type=F32)
    y = ga_ref[...].astype(F32) * ya + gr_ref[...].astype(F32) * yr
    upd = jnp.dot(y.astype(BF16), wo_ref[...], preferred_element_type=F32)
    o_ref[...] = x_ref[...] + ada_ref[0][2:3] * upd


def _merge(oa, orr, ga, gr, x2d, ada, wa, wr, wo, *, seq, tm):
    t = x2d.shape[0]
    tiles_per_batch = seq // tm
    row = lambda i: (i, 0)
    wspec = pl.BlockSpec((D_MODEL, D_MODEL), lambda i: (0, 0))
    return pl.pallas_call(
        _merge_kernel,
        grid=(t // tm,),
        in_specs=[pl.BlockSpec((tm, D_MODEL), row)] * 5
                 + [pl.BlockSpec((1, 8, D_MODEL), lambda i: (i // tiles_per_batch, 0, 0)),
                    wspec, wspec, wspec],
        out_specs=pl.BlockSpec((tm, D_MODEL), row),
        out_shape=jax.ShapeDtypeStruct((t, D_MODEL), F32),
        compiler_params=pltpu.CompilerParams(
            dimension_semantics=("arbitrary",), vmem_limit_bytes=VMEM_LIMIT),
        name="merge",
    )(oa, orr, ga, gr, x2d, ada, wa, wr, wo)


def _mlp_kernel(x_ref, ada_ref, nw_ref, nf_ref, w1_ref, w2_ref, o_ref):
    x = x_ref[...]
    ada = ada_ref[0]
    h = (_rms(x) * nw_ref[...] * (1.0 + ada[4:5]) + ada[3:4]).astype(BF16)
    acc = jnp.zeros_like(x)
    for c in range(D_FF // D_MODEL):
        cols = slice(c * D_MODEL, (c + 1) * D_MODEL)
        u = jnp.dot(h, w1_ref[:, cols], preferred_element_type=F32)
        u = jnp.square(jnp.maximum(u, 0.0))
        acc = acc + jnp.dot(u.astype(BF16), w2_ref[cols, :], preferred_element_type=F32)
    o_ref[...] = _rms(x + ada[5:6] * acc) * nf_ref[...]


def _mlp(x1, ada, norm_w, norm_final, w1, w2, *, seq, tm):
    t = x1.shape[0]
    tiles_per_batch = seq // tm
    row = lambda i: (i, 0)
    const = lambda i: (0, 0)
    return pl.pallas_call(
        _mlp_kernel,
        grid=(t // tm,),
        in_specs=[pl.BlockSpec((tm, D_MODEL), row),
                  pl.BlockSpec((1, 8, D_MODEL), lambda i: (i // tiles_per_batch, 0, 0)),
                  pl.BlockSpec((1, D_MODEL), const),
                  pl.BlockSpec((1, D_MODEL), const),
                  pl.BlockSpec((D_MODEL, D_FF), const, pipeline_mode=pl.Buffered(1)),
                  pl.BlockSpec((D_FF, D_MODEL), const, pipeline_mode=pl.Buffered(1))],
        out_specs=pl.BlockSpec((tm, D_MODEL), row),
        out_shape=jax.ShapeDtypeStruct((t, D_MODEL), F32),
        compiler_params=pltpu.CompilerParams(
            dimension_semantics=("arbitrary",), vmem_limit_bytes=VMEM_LIMIT),
        name="mlp",
    )(x1, ada, norm_w, norm_final, w1, w2)


def _rope_inv_freq_lanes():
    inv_freq = ROPE_THETA ** (-jnp.arange(0, ROPE_DIM, 2, dtype=F32) / ROPE_DIM)
    lane = jnp.arange(HEAD_W) % QK_DIM
    table = jnp.where(lane < ROPE_DIM, inv_freq[lane % (ROPE_DIM // 2)], 0.0)
    return table.reshape(1, HEAD_W).astype(F32)


def kernel(x, c, positions, w_ada, b_ada, norm_mix, w_in, lam_q1, lam_k1, lam_q2, lam_k2, subln_w, lb_logits, rec_norm_w, w_proj_att, w_proj_rec, w_out, norm_mlp, w_mlp_in, w_mlp_out, norm_final):
    batch, seq, d = x.shape
    assert d == D_MODEL and w_ada.shape[0] == 1, "single-layer, d_model=1024 only"
    assert batch <= 8
    t = batch * seq
    tm = min(512, seq)
    tq = min(512, seq)
    tt = min(256, seq)
    assert seq % tm == 0 and seq % tq == 0 and seq % tt == 0 and tt % REC_CHUNK == 0

    x2d = x.reshape(t, D_MODEL)
    c_pad = jnp.zeros((8, D_MODEL), F32).at[:batch].set(c)
    ada = _ada(c_pad, w_ada[0], b_ada[0].reshape(1, -1))[:batch]
    ada = jnp.pad(ada.reshape(batch, N_ADA, D_MODEL), ((0, 0), (0, 8 - N_ADA), (0, 0)))

    q, k, v, rq, g, ri, gate, ga, gr = _inproj(
        x2d, ada, norm_mix[0].reshape(1, -1), w_in[0].astype(BF16),
        positions.reshape(t, 1), _rope_inv_freq_lanes(), lb_logits,
        jnp.tile(rec_norm_w.reshape(1, HEAD_W), (1, HEADS)), seq=seq, tm=tm)

    as3 = lambda a: a.reshape(batch, seq, D_MODEL)
    o_a = _attn(as3(q), as3(k), as3(v), lam_q1, lam_k1, lam_q2, lam_k2,
                subln_w.reshape(HEAD_W, 1), tq=tq)
    o_r = _hgrn(rq, g, ri, gate, batch=batch, seq=seq, tt=tt)

    x1 = _merge(o_a.reshape(t, D_MODEL), o_r, ga, gr, x2d, ada,
                w_proj_att[0].astype(BF16), w_proj_rec[0].astype(BF16), w_out[0].astype(BF16),
                seq=seq, tm=tm)
    out = _mlp(x1, ada, norm_mlp[0].reshape(1, -1), norm_final.reshape(1, -1),
               w_mlp_in[0].astype(BF16), w_mlp_out[0].astype(BF16), seq=seq, tm=tm)
    return out.reshape(batch, seq, D_MODEL)
```

```python
import functools
import math

import jax
import jax.numpy as jnp
from jax import lax
from jax.experimental import pallas as pl
from jax.experimental.pallas import tpu as pltpu

F32 = jnp.float32
BF16 = jnp.bfloat16

D_MODEL = 1024
HEADS = 8
HEAD_W = 128
QK_DIM = 64
ROPE_DIM = 16
ROPE_THETA = 500000.0
D_FF = 4 * D_MODEL
N_ADA = 6
N_SECTIONS = 9
REC_CHUNK = 64
EPS = 1e-6
LAM_INIT = 0.8 - 0.6 * math.exp(-0.3 * 0)
LOG2_E = 1.0 / math.log(2.0)
SAFE_DECAY = 115.0
MAX_RISE = 64.0
SUB_KEYS = 256
STREAM_TILES = 4
QK_AHEAD = 2
HEADS_PER_STEP = 4
VMEM_LIMIT = 56 * 1024 * 1024

_NT = (((1,), (1,)), ((), ()))
_TN = (((0,), (0,)), ((), ()))


def _rms(x):
    return x * lax.rsqrt(jnp.mean(x * x, axis=-1, keepdims=True) + EPS)


def _ada_kernel(c_ref, w_ref, b_ref, o_ref):
    c = c_ref[...]
    cond = c * jax.nn.sigmoid(c)
    o_ref[...] = jnp.dot(cond, w_ref[...], preferred_element_type=F32,
                         precision=lax.Precision.HIGHEST) + b_ref[...]


def _ada(c_pad, w_ada, b_ada):
    n = w_ada.shape[1]
    return pl.pallas_call(
        _ada_kernel,
        grid=(n // D_MODEL,),
        in_specs=[pl.BlockSpec((8, D_MODEL), lambda j: (0, 0)),
                  pl.BlockSpec((D_MODEL, D_MODEL), lambda j: (0, j)),
                  pl.BlockSpec((1, D_MODEL), lambda j: (0, j))],
        out_specs=pl.BlockSpec((8, D_MODEL), lambda j: (0, j)),
        out_shape=jax.ShapeDtypeStruct((8, n), F32),
        name="ada",
    )(c_pad, w_ada, b_ada)


def _inproj_kernel(x_ref, ada_ref, nw_ref, w_ref, pos_ref, invf_ref, lb_ref, rnw_ref,
                   q_ref, k_ref, v_ref, rq_ref, g_ref, ri_ref, gate_ref, ga_ref, gr_ref):
    ada = ada_ref[0]
    h = (_rms(x_ref[...]) * nw_ref[...] * (1.0 + ada[1:2]) + ada[0:1]).astype(BF16)

    ang = pos_ref[...].astype(F32) * invf_ref[...]
    lane = lax.broadcasted_iota(jnp.int32, (1, HEAD_W), 1) % QK_DIM
    cos_a = jnp.cos(ang)
    sin_a = jnp.sin(ang)
    cs = jnp.where(lane < ROPE_DIM, cos_a, 1.0)
    s1 = jnp.where(lane < ROPE_DIM // 2, -sin_a, 0.0)
    s2 = jnp.where((lane >= ROPE_DIM // 2) & (lane < ROPE_DIM), sin_a, 0.0)

    def section(j):
        return jnp.dot(h, w_ref[:, j * D_MODEL:(j + 1) * D_MODEL], preferred_element_type=F32)

    def rope_store(ref, acc, scale):
        c_, s1_, s2_ = cs * scale, s1 * scale, s2 * scale
        for hh in range(HEADS):
            t = acc[:, hh * HEAD_W:(hh + 1) * HEAD_W]
            r = (t * c_ + pltpu.roll(t, HEAD_W - ROPE_DIM // 2, 1) * s1_
                 + pltpu.roll(t, ROPE_DIM // 2, 1) * s2_)
            ref[:, hh * HEAD_W:(hh + 1) * HEAD_W] = r.astype(ref.dtype)

    v_ref[...] = section(2).astype(v_ref.dtype)
    rope_store(q_ref, section(0), QK_DIM ** -0.5 * LOG2_E)
    rq_ref[...] = section(3).astype(rq_ref.dtype)
    rope_store(k_ref, section(1), 1.0)
    ri_ref[...] = section(5).astype(ri_ref.dtype)

    l0 = lb_ref[0:1, :]
    l1 = lb_ref[1:2, :]
    mx = jnp.maximum(l0, l1)
    e0 = jnp.exp(l0 - mx)
    e1 = jnp.exp(l1 - mx)
    lb = e1 / (e0 + e1)
    g_ref[...] = jnp.log(lb + (1.0 - lb) * jax.nn.sigmoid(section(4))) * LOG2_E

    rg = section(6)
    gate_ref[...] = (rnw_ref[...] * (rg * jax.nn.sigmoid(rg))).astype(gate_ref.dtype)
    ga_ref[...] = jax.nn.sigmoid(section(7)).astype(ga_ref.dtype)
    gr_ref[...] = jax.nn.sigmoid(section(8)).astype(gr_ref.dtype)


def _inproj(x2d, ada, norm_w, w_in, pos, invf, lb_logits, rec_norm_row, *, seq, tm):
    t = x2d.shape[0]
    tiles_per_batch = seq // tm
    row = lambda i: (i, 0)
    const2 = lambda i: (0, 0)
    out_dtypes = (BF16, BF16, BF16, BF16, F32, BF16, BF16, BF16, BF16)
    return pl.pallas_call(
        _inproj_kernel,
        grid=(t // tm,),
        in_specs=[pl.BlockSpec((tm, D_MODEL), row),
                  pl.BlockSpec((1, 8, D_MODEL), lambda i: (i // tiles_per_batch, 0, 0)),
                  pl.BlockSpec((1, D_MODEL), const2),
                  pl.BlockSpec((D_MODEL, N_SECTIONS * D_MODEL), const2, pipeline_mode=pl.Buffered(1)),
                  pl.BlockSpec((tm, 1), row),
                  pl.BlockSpec((1, HEAD_W), const2),
                  pl.BlockSpec((2, D_MODEL), const2),
                  pl.BlockSpec((1, D_MODEL), const2)],
        out_specs=[pl.BlockSpec((tm, D_MODEL), row) for _ in out_dtypes],
        out_shape=[jax.ShapeDtypeStruct((t, D_MODEL), dt) for dt in out_dtypes],
        compiler_params=pltpu.CompilerParams(
            dimension_semantics=("arbitrary",), vmem_limit_bytes=VMEM_LIMIT),
        name="inproj",
    )(x2d, ada, norm_w, w_in, pos, invf, lb_logits, rec_norm_row)


def _attn_kernel(lq1_ref, lk1_ref, lq2_ref, lk2_ref, sw_ref, q_ref, k_ref, v_ref, o_ref,
                 vt_scr, acc, m, l, dev, *, tq):
    qi = pl.program_id(2)
    n_tiles = pl.num_programs(2)
    n_streams = 2 * HEADS_PER_STEP

    @pl.when(qi == 0)
    def _():
        def tr(j, carry):
            start = pl.multiple_of(j * tq, tq)
            vt_scr[:, pl.ds(start, tq)] = v_ref[0, pl.ds(start, tq), :].T
            return carry
        lax.fori_loop(0, n_tiles, tr, 0)

    def head_lanes(s):
        return slice((s // 2) * HEAD_W, (s // 2 + 1) * HEAD_W)

    q = q_ref[0]
    lane = lax.broadcasted_iota(jnp.int32, (1, HEAD_W), 1)
    q_streams = []
    for s in range(n_streams):
        qh = q[:, head_lanes(s)]
        keep = (lane < QK_DIM) if s % 2 == 0 else (lane >= QK_DIM)
        q_streams.append(jnp.where(keep, qh, jnp.zeros_like(qh)))

    def init():
        acc[...] = jnp.zeros_like(acc)
        m[...] = jnp.full_like(m, -jnp.inf)
        l[...] = jnp.zeros_like(l)
        dev[...] = jnp.zeros_like(dev)

    def tiles(kj):
        start = pl.multiple_of(kj * tq, tq)
        return k_ref[0, pl.ds(start, tq), :], vt_scr[:, pl.ds(start, tq)]

    def exact_step(kj):
        k, vt = tiles(kj)
        for s in range(n_streams):
            hl = head_lanes(s)
            st = lax.dot_general(k[:, hl], q_streams[s], _NT, preferred_element_type=F32)
            kr = lax.broadcasted_iota(jnp.int32, st.shape, 0) + (kj - qi) * tq
            qc = lax.broadcasted_iota(jnp.int32, st.shape, 1)
            st = jnp.where(kr <= qc, st, -jnp.inf)
            m_prev = m[s]
            m_new = jnp.maximum(m_prev, jnp.max(st, axis=0, keepdims=True))
            alpha = jnp.exp2(m_prev - m_new)
            pt = jnp.exp2(st - m_new)
            l[s] = alpha * l[s] + jnp.sum(pt, axis=0, keepdims=True)
            acc[s] = alpha * acc[s] + jnp.dot(vt[hl, :], pt.astype(BF16), preferred_element_type=F32)
            m[s] = m_new

    def diag_step():
        k, vt = tiles(qi)
        half = tq // 2
        problems = ((slice(0, half), slice(0, half)), (slice(0, tq), slice(half, tq)))
        sts = {(s, pi): lax.dot_general(k[ks, head_lanes(s)], q_streams[s][qs], _NT,
                                        preferred_element_type=F32)
               for s in range(n_streams) for pi, (ks, qs) in enumerate(problems)}
        dev[...] = jnp.zeros_like(dev)
        for s in range(n_streams):
            for pi, (ks, qs) in enumerate(problems):
                st = sts[s, pi]
                kr = lax.broadcasted_iota(jnp.int32, st.shape, 0)
                qc = lax.broadcasted_iota(jnp.int32, st.shape, 1) + qs.start
                st = jnp.where(kr <= qc, st, -jnp.inf)
                m_new = jnp.max(st, axis=0, keepdims=True)
                pt = jnp.exp2(st - m_new)
                l[s, :, qs] = jnp.sum(pt, axis=0, keepdims=True)
                acc[s, :, qs] = jnp.dot(vt[head_lanes(s), ks], pt.astype(BF16),
                                        preferred_element_type=F32)
                m[s, :, qs] = m_new

    def stream_tiles(kj0, n):
        kv = [tiles(kj0 + t) for t in range(n)]
        n_sub = tq // SUB_KEYS
        chains = [(t, s, h) for t in range(n) for s in range(n_streams) for h in range(n_sub)]

        def qk(c):
            t, s, h = c
            ks = slice(h * SUB_KEYS, (h + 1) * SUB_KEYS)
            return lax.dot_general(kv[t][0][ks, head_lanes(s)], q_streams[s], _NT,
                                   preferred_element_type=F32)

        scores = {i: qk(c) for i, c in enumerate(chains[:QK_AHEAD])}
        part = {}
        for i, (t, s, h) in enumerate(chains):
            st = scores.pop(i)
            m_prev = m[s]
            pt = jnp.exp2(st - m_prev)
            mx = jnp.max(st, axis=0, keepdims=True)
            sm = jnp.sum(pt, axis=0, keepdims=True)
            if i + QK_AHEAD < len(chains):
                scores[i + QK_AHEAD] = qk(chains[i + QK_AHEAD])
            ks = slice(h * SUB_KEYS, (h + 1) * SUB_KEYS)
            pp = jnp.dot(kv[t][1][head_lanes(s), ks], pt.astype(BF16), preferred_element_type=F32)
            if h == 0:
                part[s] = (mx, sm, pp)
            else:
                mx0, sm0, pp0 = part[s]
                part[s] = (jnp.maximum(mx0, mx), sm0 + sm, pp0 + pp)
            if h == n_sub - 1:
                t_max, l_add, pv = part.pop(s)
                m_new = jnp.maximum(m_prev, t_max)
                alpha = jnp.exp2(m_prev - m_new)
                dev[s] = jnp.maximum(dev[s], t_max - m_prev)
                l[s] = alpha * (l[s] + l_add)
                acc[s] = alpha * (acc[s] + pv)
                m[s] = m_new

    diag_step()

    def body(t, carry):
        stream_tiles(STREAM_TILES * t, STREAM_TILES)
        return carry

    lax.fori_loop(0, qi // STREAM_TILES, body, 0)

    for r in range(1, STREAM_TILES):
        @pl.when(qi % STREAM_TILES == r)
        def _(r=r):
            stream_tiles(qi - r, r)

    lam = (jnp.exp(jnp.sum(lq1_ref[...] * lk1_ref[...], axis=-1, keepdims=True))
           - jnp.exp(jnp.sum(lq2_ref[...] * lk2_ref[...], axis=-1, keepdims=True)) + LAM_INIT)

    def write_output():
        for hh in range(HEADS_PER_STEP):
            s1, s2 = 2 * hh, 2 * hh + 1
            ot = acc[s1] / l[s1] - lam * (acc[s2] / l[s2])
            inv = lax.rsqrt(jnp.mean(ot * ot, axis=0, keepdims=True) + EPS)
            ot = ot * inv * sw_ref[...] * (1.0 - LAM_INIT)
            o_ref[0, hh * HEAD_W:(hh + 1) * HEAD_W, :] = ot.astype(o_ref.dtype)

    write_output()

    rise = jnp.max(dev[...])

    @pl.when(jnp.logical_not(rise <= MAX_RISE))
    def _():
        init()

        def redo(kj, carry):
            exact_step(kj)
            return carry
        lax.fori_loop(0, qi + 1, redo, 0)
        write_output()


def _attn(q3, k3, v3, lq1, lk1, lq2, lk2, subln_col, *, tq):
    b, s, _ = q3.shape
    vec = lambda bi, h, qi: (0, 0)
    width = HEADS_PER_STEP * HEAD_W
    n_streams = 2 * HEADS_PER_STEP
    kernel = functools.partial(_attn_kernel, tq=tq)
    return pl.pallas_call(
        kernel,
        grid=(b, HEADS // HEADS_PER_STEP, s // tq),
        in_specs=[pl.BlockSpec((1, QK_DIM), vec)] * 4
                 + [pl.BlockSpec((HEAD_W, 1), vec),
                    pl.BlockSpec((1, tq, width), lambda bi, h, qi: (bi, qi, h)),
                    pl.BlockSpec((1, s, width), lambda bi, h, qi: (bi, 0, h)),
                    pl.BlockSpec((1, s, width), lambda bi, h, qi: (bi, 0, h))],
        out_specs=pl.BlockSpec((1, width, tq), lambda bi, h, qi: (bi, h, qi)),
        out_shape=jax.ShapeDtypeStruct((b, D_MODEL, s), BF16),
        scratch_shapes=[pltpu.VMEM((width, s), BF16),
                        pltpu.VMEM((n_streams, HEAD_W, tq), F32),
                        pltpu.VMEM((n_streams, 1, tq), F32),
                        pltpu.VMEM((n_streams, 1, tq), F32),
                        pltpu.VMEM((n_streams, 1, tq), F32)],
        compiler_params=pltpu.CompilerParams(
            dimension_semantics=("arbitrary", "arbitrary", "arbitrary"), vmem_limit_bytes=VMEM_LIMIT),
        name="attn",
    )(lq1, lk1, lq2, lk2, subln_col, q3, k3, v3)


def _hgrn_kernel(rq_ref, g_ref, ri_ref, gate_ref, o_ref, st_scr, sc_scr, *, n_chunks):
    c_len = REC_CHUNK

    @pl.when(pl.program_id(1) == 0)
    def _():
        st_scr[...] = jnp.zeros_like(st_scr)

    rows_i = lax.broadcasted_iota(jnp.int32, (c_len, c_len), 0)
    cols_i = lax.broadcasted_iota(jnp.int32, (c_len, c_len), 1)
    causal = rows_i >= cols_i
    row_col = lax.broadcasted_iota(jnp.int32, (c_len, 1), 0)

    n_rows = n_chunks * c_len
    t_row = lax.broadcasted_iota(jnp.int32, (n_rows, n_rows), 0)
    t_col = lax.broadcasted_iota(jnp.int32, (n_rows, n_rows), 1)
    tri = ((t_row // c_len == t_col // c_len) & (t_row >= t_col)).astype(BF16)
    g_all = g_ref[...]
    g_hi = g_all.astype(BF16)
    rest = g_all - g_hi.astype(F32)
    g_mid = rest.astype(BF16)
    g_lo = (rest - g_mid.astype(F32)).astype(BF16)
    b_all = (jnp.dot(tri, g_hi, preferred_element_type=F32)
             + jnp.dot(tri, g_mid, preferred_element_type=F32)
             + jnp.dot(tri, g_lo, preferred_element_type=F32))

    def prepare(c):
        rows = slice(c * c_len, (c + 1) * c_len)
        g = g_all[rows]
        b = b_all[rows]
        b_last = b[c_len - 1:c_len, :]
        b_mid = b[c_len // 2 - 1:c_len // 2, :]
        safe = jnp.maximum(jnp.max(-b_mid), jnp.max(b_mid - b_last)) < SAFE_DECAY
        return rows, g, b, b_last, b_mid, safe

    def factored_scores(q, kk, b, b_mid):
        qb = (q * jnp.exp2(b - b_mid)).astype(BF16)
        kh = (kk * jnp.exp2(b_mid - b)).astype(BF16)

        def head(h):
            sl = slice(h * HEAD_W, (h + 1) * HEAD_W)
            s = lax.dot_general(qb[:, sl], kh[:, sl], _NT, preferred_element_type=F32)
            return jnp.where(causal, s, 0.0)
        return head

    def exact_scores(q, kk, b):
        def head(h):
            sl = slice(h * HEAD_W, (h + 1) * HEAD_W)
            qh, bh, kh = q[:, sl], b[:, sl], kk[:, sl]

            def col_body(s_idx, scores):
                pick = row_col == s_idx
                b_row = jnp.sum(jnp.where(pick, bh, 0.0), axis=0, keepdims=True)
                k_row = jnp.sum(jnp.where(pick, kh, 0.0), axis=0, keepdims=True)
                e = jnp.exp2(jnp.minimum(bh - b_row, 0.0))
                col = jnp.sum(qh * k_row * e, axis=1, keepdims=True)
                col = jnp.where(row_col >= s_idx, col, 0.0)
                return jnp.where(cols_i == s_idx, col, scores)

            return lax.fori_loop(0, c_len, col_body, jnp.zeros((c_len, c_len), F32))
        return head

    def finish(rows, q, kk, b, b_last, scores_of_head, state):
        qb = (q * jnp.exp2(b)).astype(BF16)
        kdb = (kk * jnp.exp2(b_last - b)).astype(BF16)
        decay_last = jnp.exp2(b_last)
        lanes = [slice(h * HEAD_W, (h + 1) * HEAD_W) for h in range(HEADS)]
        vs = [ri_ref[rows, sl] for sl in lanes]
        sc = [scores_of_head(h).astype(BF16) for h in range(HEADS)]
        upd = [lax.dot_general(vs[h], kdb[:, lanes[h]], _TN, preferred_element_type=F32)
               for h in range(HEADS)]
        intra = [jnp.dot(sc[h], vs[h], preferred_element_type=F32) for h in range(HEADS)]
        inter = [lax.dot_general(qb[:, lanes[h]], state[h].astype(BF16), _NT,
                                 preferred_element_type=F32) for h in range(HEADS)]
        for h, sl in enumerate(lanes):
            state[h] = state[h] * decay_last[:, sl] + upd[h]
            o = intra[h] + inter[h]
            o_ref[rows, sl] = (_rms(o) * gate_ref[rows, sl].astype(F32)).astype(o_ref.dtype)

    prepared = [prepare(c) for c in range(n_chunks)]
    all_safe = functools.reduce(jnp.logical_and, [p[5] for p in prepared])

    state = [st_scr[h] for h in range(HEADS)]
    for rows, g, b, b_last, b_mid, _ in prepared:
        q = rq_ref[rows, :].astype(F32)
        kk = 1.0 - jnp.exp2(g)
        finish(rows, q, kk, b, b_last, factored_scores(q, kk, b, b_mid), state)

    @pl.when(all_safe)
    def _():
        for h in range(HEADS):
            st_scr[h] = state[h]

    @pl.when(jnp.logical_not(all_safe))
    def _():
        redo_state = [st_scr[h] for h in range(HEADS)]
        for rows, g, b, b_last, b_mid, safe in prepared:
            q = rq_ref[rows, :].astype(F32)
            kk = 1.0 - jnp.exp2(g)

            @pl.when(safe)
            def _():
                head = factored_scores(q, kk, b, b_mid)
                for h in range(HEADS):
                    sc_scr[h] = head(h)

            @pl.when(jnp.logical_not(safe))
            def _():
                head = exact_scores(q, kk, b)
                for h in range(HEADS):
                    sc_scr[h] = head(h)

            finish(rows, q, kk, b, b_last, lambda h: sc_scr[h], redo_state)
        for h in range(HEADS):
            st_scr[h] = redo_state[h]


def _hgrn(rq, g, ri, gate, *, batch, seq, tt):
    t = rq.shape[0]
    tiles = seq // tt
    row = lambda bi, ti: (bi * tiles + ti, 0)
    kernel = functools.partial(_hgrn_kernel, n_chunks=tt // REC_CHUNK)
    return pl.pallas_call(
        kernel,
        grid=(batch, tiles),
        in_specs=[pl.BlockSpec((tt, D_MODEL), row)] * 4,
        out_specs=pl.BlockSpec((tt, D_MODEL), row),
        out_shape=jax.ShapeDtypeStruct((t, D_MODEL), BF16),
        scratch_shapes=[pltpu.VMEM((HEADS, HEAD_W, HEAD_W), F32),
                        pltpu.VMEM((HEADS, REC_CHUNK, REC_CHUNK), F32)],
        compiler_params=pltpu.CompilerParams(
            dimension_semantics=("arbitrary", "arbitrary"), vmem_limit_bytes=VMEM_LIMIT),
        name="hgrn",
    )(rq, g, ri, gate)


def _merge_kernel(oa_ref, or_ref, ga_ref, gr_ref, x_ref, ada_ref, wa_ref, wr_ref, wo_ref, o_ref):
    ya = lax.dot_general(oa_ref[0], wa_ref[...], _TN, preferred_element_type=F32)
    yr = jnp.dot(or_ref[...], wr_ref[...], preferred_element_type=F32)
    y = ga_ref[...].astype(F32) * ya + gr_ref[...].astype(F32) * yr
    upd = jnp.dot(y.astype(BF16), wo_ref[...], preferred_element_type=F32)
    o_ref[...] = x_ref[...] + ada_ref[0][2:3] * upd


def _merge(oa, orr, ga, gr, x2d, ada, wa, wr, wo, *, seq, tm):
    t = x2d.shape[0]
    tiles_per_batch = seq // tm
    row = lambda i: (i, 0)
    wspec = pl.BlockSpec((D_MODEL, D_MODEL), lambda i: (0, 0))
    return pl.pallas_call(
        _merge_kernel,
        grid=(t // tm,),
        in_specs=[pl.BlockSpec((1, D_MODEL, tm),
                               lambda i: (i // tiles_per_batch, 0, i % tiles_per_batch))]
                 + [pl.BlockSpec((tm, D_MODEL), row)] * 4
                 + [pl.BlockSpec((1, 8, D_MODEL), lambda i: (i // tiles_per_batch, 0, 0)),
                    wspec, wspec, wspec],
        out_specs=pl.BlockSpec((tm, D_MODEL), row),
        out_shape=jax.ShapeDtypeStruct((t, D_MODEL), F32),
        compiler_params=pltpu.CompilerParams(
            dimension_semantics=("arbitrary",), vmem_limit_bytes=VMEM_LIMIT),
        name="merge",
    )(oa, orr, ga, gr, x2d, ada, wa, wr, wo)


def _mlp_kernel(x_ref, ada_ref, nw_ref, nf_ref, w1_ref, w2_ref, o_ref):
    x = x_ref[...]
    ada = ada_ref[0]
    h = (_rms(x) * nw_ref[...] * (1.0 + ada[4:5]) + ada[3:4]).astype(BF16)
    acc = jnp.zeros_like(x)
    for c in range(D_FF // D_MODEL):
        cols = slice(c * D_MODEL, (c + 1) * D_MODEL)
        u = jnp.dot(h, w1_ref[:, cols], preferred_element_type=F32)
        u = jnp.square(jnp.maximum(u, 0.0))
        acc = acc + jnp.dot(u.astype(BF16), w2_ref[cols, :], preferred_element_type=F32)
    o_ref[...] = _rms(x + ada[5:6] * acc) * nf_ref[...]


def _mlp(x1, ada, norm_w, norm_final, w1, w2, *, seq, tm):
    t = x1.shape[0]
    tiles_per_batch = seq // tm
    row = lambda i: (i, 0)
    const = lambda i: (0, 0)
    return pl.pallas_call(
        _mlp_kernel,
        grid=(t // tm,),
        in_specs=[pl.BlockSpec((tm, D_MODEL), row),
                  pl.BlockSpec((1, 8, D_MODEL), lambda i: (i // tiles_per_batch, 0, 0)),
                  pl.BlockSpec((1, D_MODEL), const),
                  pl.BlockSpec((1, D_MODEL), const),
                  pl.BlockSpec((D_MODEL, D_FF), const, pipeline_mode=pl.Buffered(1)),
                  pl.BlockSpec((D_FF, D_MODEL), const, pipeline_mode=pl.Buffered(1))],
        out_specs=pl.BlockSpec((tm, D_MODEL), row),
        out_shape=jax.ShapeDtypeStruct((t, D_MODEL), F32),
        compiler_params=pltpu.CompilerParams(
            dimension_semantics=("arbitrary",), vmem_limit_bytes=VMEM_LIMIT),
        name="mlp",
    )(x1, ada, norm_w, norm_final, w1, w2)


def _rope_inv_freq_lanes():
    inv_freq = ROPE_THETA ** (-jnp.arange(0, ROPE_DIM, 2, dtype=F32) / ROPE_DIM)
    lane = jnp.arange(HEAD_W) % QK_DIM
    table = jnp.where(lane < ROPE_DIM, inv_freq[lane % (ROPE_DIM // 2)], 0.0)
    return table.reshape(1, HEAD_W).astype(F32)


def kernel(x, c, positions, w_ada, b_ada, norm_mix, w_in, lam_q1, lam_k1, lam_q2, lam_k2, subln_w, lb_logits, rec_norm_w, w_proj_att, w_proj_rec, w_out, norm_mlp, w_mlp_in, w_mlp_out, norm_final):
    batch, seq, d = x.shape
    assert d == D_MODEL and w_ada.shape[0] == 1, "single-layer, d_model=1024 only"
    assert batch <= 8
    t = batch * seq
    tm = min(512, seq)
    tq = min(512, seq)
    tt = min(256, seq)
    assert seq % tm == 0 and seq % tq == 0 and seq % tt == 0 and tt % REC_CHUNK == 0

    x2d = x.reshape(t, D_MODEL)
    c_pad = jnp.zeros((8, D_MODEL), F32).at[:batch].set(c)
    ada = _ada(c_pad, w_ada[0], b_ada[0].reshape(1, -1))[:batch]
    ada = jnp.pad(ada.reshape(batch, N_ADA, D_MODEL), ((0, 0), (0, 8 - N_ADA), (0, 0)))

    q, k, v, rq, g, ri, gate, ga, gr = _inproj(
        x2d, ada, norm_mix[0].reshape(1, -1), w_in[0].astype(BF16),
        positions.reshape(t, 1), _rope_inv_freq_lanes(), lb_logits,
        jnp.tile(rec_norm_w.reshape(1, HEAD_W), (1, HEADS)), seq=seq, tm=tm)

    as3 = lambda a: a.reshape(batch, seq, D_MODEL)
    o_a = _attn(as3(q), as3(k), as3(v), lam_q1, lam_k1, lam_q2, lam_k2,
                subln_w.reshape(HEAD_W, 1), tq=tq)
    o_r = _hgrn(rq, g, ri, gate, batch=batch, seq=seq, tt=tt)

    x1 = _merge(o_a, o_r, ga, gr, x2d, ada,
                w_proj_att[0].astype(BF16), w_proj_rec[0].astype(BF16), w_out[0].astype(BF16),
                seq=seq, tm=tm)
    out = _mlp(x1, ada, norm_mlp[0].reshape(1, -1), norm_final.reshape(1, -1),
               w_mlp_in[0].astype(BF16), w_mlp_out[0].astype(BF16), seq=seq, tm=tm)
    return out.reshape(batch, seq, D_MODEL)
```

```python
import functools
import math

import jax
import jax.numpy as jnp
from jax import lax
from jax.experimental import pallas as pl
from jax.experimental.pallas import tpu as pltpu

F32 = jnp.float32
BF16 = jnp.bfloat16

D_MODEL = 1024
HEADS = 8
HEAD_W = 128
QK_DIM = 64
ROPE_DIM = 16
ROPE_THETA = 500000.0
D_FF = 4 * D_MODEL
N_ADA = 6
N_SECTIONS = 9
REC_CHUNK = 64
EPS = 1e-6
LAM_INIT = 0.8 - 0.6 * math.exp(-0.3 * 0)
LOG2_E = 1.0 / math.log(2.0)
SAFE_DECAY = 115.0
MAX_RISE = 64.0
SUB_KEYS = 256
STREAM_TILES = 4
QK_AHEAD = 2
HEADS_PER_STEP = 4
ROW_GROUPS = 2
VMEM_LIMIT = 56 * 1024 * 1024

_NT = (((1,), (1,)), ((), ()))
_TN = (((0,), (0,)), ((), ()))


def _rms(x):
    return x * lax.rsqrt(jnp.mean(x * x, axis=-1, keepdims=True) + EPS)


def _ada_kernel(c_ref, w_ref, b_ref, o_ref):
    c = c_ref[...]
    cond = c * jax.nn.sigmoid(c)
    o_ref[...] = jnp.dot(cond, w_ref[...], preferred_element_type=F32,
                         precision=lax.Precision.HIGHEST) + b_ref[...]


def _ada(c_pad, w_ada, b_ada):
    n = w_ada.shape[1]
    return pl.pallas_call(
        _ada_kernel,
        grid=(n // D_MODEL,),
        in_specs=[pl.BlockSpec((8, D_MODEL), lambda j: (0, 0)),
                  pl.BlockSpec((D_MODEL, D_MODEL), lambda j: (0, j)),
                  pl.BlockSpec((1, D_MODEL), lambda j: (0, j))],
        out_specs=pl.BlockSpec((8, D_MODEL), lambda j: (0, j)),
        out_shape=jax.ShapeDtypeStruct((8, n), F32),
        name="ada",
    )(c_pad, w_ada, b_ada)


def _inproj_kernel(x_ref, ada_ref, nw_ref, w_ref, pos_ref, invf_ref, lb_ref, rnw_ref,
                   q_ref, k_ref, v_ref, rq_ref, g_ref, ri_ref, gate_ref, ga_ref, gr_ref):
    ada = ada_ref[0]
    rows = x_ref.shape[0] // ROW_GROUPS
    hs = [(_rms(x_ref[r * rows:(r + 1) * rows, :]) * nw_ref[...] * (1.0 + ada[1:2]) + ada[0:1]).astype(BF16)
          for r in range(ROW_GROUPS)]

    ang = pos_ref[...].astype(F32) * invf_ref[...]
    lane = lax.broadcasted_iota(jnp.int32, (1, HEAD_W), 1) % QK_DIM
    cos_a = jnp.cos(ang)
    sin_a = jnp.sin(ang)
    cs = jnp.where(lane < ROPE_DIM, cos_a, 1.0)
    s1 = jnp.where(lane < ROPE_DIM // 2, -sin_a, 0.0)
    s2 = jnp.where((lane >= ROPE_DIM // 2) & (lane < ROPE_DIM), sin_a, 0.0)

    def section(j):
        w = w_ref[:, j * D_MODEL:(j + 1) * D_MODEL]
        return jnp.concatenate([jnp.dot(hh, w, preferred_element_type=F32) for hh in hs], axis=0)

    def rope_store(ref, acc, scale):
        c_, s1_, s2_ = cs * scale, s1 * scale, s2 * scale
        for hh in range(HEADS):
            t = acc[:, hh * HEAD_W:(hh + 1) * HEAD_W]
            r = (t * c_ + pltpu.roll(t, HEAD_W - ROPE_DIM // 2, 1) * s1_
                 + pltpu.roll(t, ROPE_DIM // 2, 1) * s2_)
            ref[:, hh * HEAD_W:(hh + 1) * HEAD_W] = r.astype(ref.dtype)

    v_ref[...] = section(2).astype(v_ref.dtype)
    rope_store(q_ref, section(0), QK_DIM ** -0.5 * LOG2_E)
    rq_ref[...] = section(3).astype(rq_ref.dtype)
    rope_store(k_ref, section(1), 1.0)
    ri_ref[...] = section(5).astype(ri_ref.dtype)

    l0 = lb_ref[0:1, :]
    l1 = lb_ref[1:2, :]
    mx = jnp.maximum(l0, l1)
    e0 = jnp.exp(l0 - mx)
    e1 = jnp.exp(l1 - mx)
    lb = e1 / (e0 + e1)
    g_ref[...] = jnp.log(lb + (1.0 - lb) * jax.nn.sigmoid(section(4))) * LOG2_E

    rg = section(6)
    gate_ref[...] = (rnw_ref[...] * (rg * jax.nn.sigmoid(rg))).astype(gate_ref.dtype)
    ga_ref[...] = jax.nn.sigmoid(section(7)).astype(ga_ref.dtype)
    gr_ref[...] = jax.nn.sigmoid(section(8)).astype(gr_ref.dtype)


def _inproj(x2d, ada, norm_w, w_in, pos, invf, lb_logits, rec_norm_row, *, seq, tm):
    t = x2d.shape[0]
    tiles_per_batch = seq // tm
    row = lambda i: (i, 0)
    const2 = lambda i: (0, 0)
    out_dtypes = (BF16, BF16, BF16, BF16, F32, BF16, BF16, BF16, BF16)
    return pl.pallas_call(
        _inproj_kernel,
        grid=(t // tm,),
        in_specs=[pl.BlockSpec((tm, D_MODEL), row),
                  pl.BlockSpec((1, 8, D_MODEL), lambda i: (i // tiles_per_batch, 0, 0)),
                  pl.BlockSpec((1, D_MODEL), const2),
                  pl.BlockSpec((D_MODEL, N_SECTIONS * D_MODEL), const2, pipeline_mode=pl.Buffered(1)),
                  pl.BlockSpec((tm, 1), row),
                  pl.BlockSpec((1, HEAD_W), const2),
                  pl.BlockSpec((2, D_MODEL), const2),
                  pl.BlockSpec((1, D_MODEL), const2)],
        out_specs=[pl.BlockSpec((tm, D_MODEL), row) for _ in out_dtypes],
        out_shape=[jax.ShapeDtypeStruct((t, D_MODEL), dt) for dt in out_dtypes],
        compiler_params=pltpu.CompilerParams(
            dimension_semantics=("arbitrary",), vmem_limit_bytes=VMEM_LIMIT),
        name="inproj",
    )(x2d, ada, norm_w, w_in, pos, invf, lb_logits, rec_norm_row)


def _attn_kernel(lq1_ref, lk1_ref, lq2_ref, lk2_ref, sw_ref, q_ref, k_ref, v_ref, o_ref,
                 vt_scr, acc, m, l, dev, *, tq):
    qi = pl.program_id(2)
    n_tiles = pl.num_programs(2)
    n_streams = 2 * HEADS_PER_STEP

    @pl.when(qi == 0)
    def _():
        def tr(j, carry):
            start = pl.multiple_of(j * tq, tq)
            vt_scr[:, pl.ds(start, tq)] = v_ref[0, pl.ds(start, tq), :].T
            return carry
        lax.fori_loop(0, n_tiles, tr, 0)

    def head_lanes(s):
        return slice((s // 2) * HEAD_W, (s // 2 + 1) * HEAD_W)

    q = q_ref[0]
    lane = lax.broadcasted_iota(jnp.int32, (1, HEAD_W), 1)
    q_streams = []
    for s in range(n_streams):
        qh = q[:, head_lanes(s)]
        keep = (lane < QK_DIM) if s % 2 == 0 else (lane >= QK_DIM)
        q_streams.append(jnp.where(keep, qh, jnp.zeros_like(qh)))

    def init():
        acc[...] = jnp.zeros_like(acc)
        m[...] = jnp.full_like(m, -jnp.inf)
        l[...] = jnp.zeros_like(l)
        dev[...] = jnp.zeros_like(dev)

    def tiles(kj):
        start = pl.multiple_of(kj * tq, tq)
        return k_ref[0, pl.ds(start, tq), :], vt_scr[:, pl.ds(start, tq)]

    def exact_step(kj):
        k, vt = tiles(kj)
        for s in range(n_streams):
            hl = head_lanes(s)
            st = lax.dot_general(k[:, hl], q_streams[s], _NT, preferred_element_type=F32)
            kr = lax.broadcasted_iota(jnp.int32, st.shape, 0) + (kj - qi) * tq
            qc = lax.broadcasted_iota(jnp.int32, st.shape, 1)
            st = jnp.where(kr <= qc, st, -jnp.inf)
            m_prev = m[s]
            m_new = jnp.maximum(m_prev, jnp.max(st, axis=0, keepdims=True))
            alpha = jnp.exp2(m_prev - m_new)
            pt = jnp.exp2(st - m_new)
            l[s] = alpha * l[s] + jnp.sum(pt, axis=0, keepdims=True)
            acc[s] = alpha * acc[s] + jnp.dot(vt[hl, :], pt.astype(BF16), preferred_element_type=F32)
            m[s] = m_new

    def diag_step():
        k, vt = tiles(qi)
        half = tq // 2
        problems = ((slice(0, half), slice(0, half)), (slice(0, tq), slice(half, tq)))
        sts = {(s, pi): lax.dot_general(k[ks, head_lanes(s)], q_streams[s][qs], _NT,
                                        preferred_element_type=F32)
               for s in range(n_streams) for pi, (ks, qs) in enumerate(problems)}
        dev[...] = jnp.zeros_like(dev)
        for s in range(n_streams):
            for pi, (ks, qs) in enumerate(problems):
                st = sts[s, pi]
                kr = lax.broadcasted_iota(jnp.int32, st.shape, 0)
                qc = lax.broadcasted_iota(jnp.int32, st.shape, 1) + qs.start
                st = jnp.where(kr <= qc, st, -jnp.inf)
                m_new = jnp.max(st, axis=0, keepdims=True)
                pt = jnp.exp2(st - m_new)
                l[s, :, qs] = jnp.sum(pt, axis=0, keepdims=True)
                acc[s, :, qs] = jnp.dot(vt[head_lanes(s), ks], pt.astype(BF16),
                                        preferred_element_type=F32)
                m[s, :, qs] = m_new

    def stream_tiles(kj0, n):
        kv = [tiles(kj0 + t) for t in range(n)]
        n_sub = tq // SUB_KEYS
        chains = [(t, s, h) for t in range(n) for s in range(n_streams) for h in range(n_sub)]

        def qk(c):
            t, s, h = c
            ks = slice(h * SUB_KEYS, (h + 1) * SUB_KEYS)
            return lax.dot_general(kv[t][0][ks, head_lanes(s)], q_streams[s], _NT,
                                   preferred_element_type=F32)

        scores = {i: qk(c) for i, c in enumerate(chains[:QK_AHEAD])}
        part = {}
        for i, (t, s, h) in enumerate(chains):
            st = scores.pop(i)
            m_prev = m[s]
            pt = jnp.exp2(st - m_prev)
            mx = jnp.max(st, axis=0, keepdims=True)
            sm = jnp.sum(pt, axis=0, keepdims=True)
            if i + QK_AHEAD < len(chains):
                scores[i + QK_AHEAD] = qk(chains[i + QK_AHEAD])
            ks = slice(h * SUB_KEYS, (h + 1) * SUB_KEYS)
            pp = jnp.dot(kv[t][1][head_lanes(s), ks], pt.astype(BF16), preferred_element_type=F32)
            if h == 0:
                part[s] = (mx, sm, pp)
            else:
                mx0, sm0, pp0 = part[s]
                part[s] = (jnp.maximum(mx0, mx), sm0 + sm, pp0 + pp)
            if h == n_sub - 1:
                t_max, l_add, pv = part.pop(s)
                m_new = jnp.maximum(m_prev, t_max)
                alpha = jnp.exp2(m_prev - m_new)
                dev[s] = jnp.maximum(dev[s], t_max - m_prev)
                l[s] = alpha * (l[s] + l_add)
                acc[s] = alpha * (acc[s] + pv)
                m[s] = m_new

    diag_step()

    def body(t, carry):
        stream_tiles(STREAM_TILES * t, STREAM_TILES)
        return carry

    lax.fori_loop(0, qi // STREAM_TILES, body, 0)

    for r in range(1, STREAM_TILES):
        @pl.when(qi % STREAM_TILES == r)
        def _(r=r):
            stream_tiles(qi - r, r)

    lam = (jnp.exp(jnp.sum(lq1_ref[...] * lk1_ref[...], axis=-1, keepdims=True))
           - jnp.exp(jnp.sum(lq2_ref[...] * lk2_ref[...], axis=-1, keepdims=True)) + LAM_INIT)

    def write_output():
        for hh in range(HEADS_PER_STEP):
            s1, s2 = 2 * hh, 2 * hh + 1
            ot = acc[s1] / l[s1] - lam * (acc[s2] / l[s2])
            inv = lax.rsqrt(jnp.mean(ot * ot, axis=0, keepdims=True) + EPS)
            ot = ot * inv * sw_ref[...] * (1.0 - LAM_INIT)
            o_ref[0, hh * HEAD_W:(hh + 1) * HEAD_W, :] = ot.astype(o_ref.dtype)

    write_output()

    rise = jnp.max(dev[...])

    @pl.when(jnp.logical_not(rise <= MAX_RISE))
    def _():
        init()

        def redo(kj, carry):
            exact_step(kj)
            return carry
        lax.fori_loop(0, qi + 1, redo, 0)
        write_output()


def _attn(q3, k3, v3, lq1, lk1, lq2, lk2, subln_col, *, tq):
    b, s, _ = q3.shape
    vec = lambda bi, h, qi: (0, 0)
    width = HEADS_PER_STEP * HEAD_W
    n_streams = 2 * HEADS_PER_STEP
    kernel = functools.partial(_attn_kernel, tq=tq)
    return pl.pallas_call(
        kernel,
        grid=(b, HEADS // HEADS_PER_STEP, s // tq),
        in_specs=[pl.BlockSpec((1, QK_DIM), vec)] * 4
                 + [pl.BlockSpec((HEAD_W, 1), vec),
                    pl.BlockSpec((1, tq, width), lambda bi, h, qi: (bi, qi, h)),
                    pl.BlockSpec((1, s, width), lambda bi, h, qi: (bi, 0, h)),
                    pl.BlockSpec((1, s, width), lambda bi, h, qi: (bi, 0, h))],
        out_specs=pl.BlockSpec((1, width, tq), lambda bi, h, qi: (bi, h, qi)),
        out_shape=jax.ShapeDtypeStruct((b, D_MODEL, s), BF16),
        scratch_shapes=[pltpu.VMEM((width, s), BF16),
                        pltpu.VMEM((n_streams, HEAD_W, tq), F32),
                        pltpu.VMEM((n_streams, 1, tq), F32),
                        pltpu.VMEM((n_streams, 1, tq), F32),
                        pltpu.VMEM((n_streams, 1, tq), F32)],
        compiler_params=pltpu.CompilerParams(
            dimension_semantics=("arbitrary", "arbitrary", "arbitrary"), vmem_limit_bytes=VMEM_LIMIT),
        name="attn",
    )(lq1, lk1, lq2, lk2, subln_col, q3, k3, v3)


def _hgrn_kernel(rq_ref, g_ref, ri_ref, gate_ref, o_ref, st_scr, sc_scr, *, n_chunks):
    c_len = REC_CHUNK

    @pl.when(pl.program_id(1) == 0)
    def _():
        st_scr[...] = jnp.zeros_like(st_scr)

    rows_i = lax.broadcasted_iota(jnp.int32, (c_len, c_len), 0)
    cols_i = lax.broadcasted_iota(jnp.int32, (c_len, c_len), 1)
    causal = rows_i >= cols_i
    row_col = lax.broadcasted_iota(jnp.int32, (c_len, 1), 0)

    n_rows = n_chunks * c_len
    t_row = lax.broadcasted_iota(jnp.int32, (n_rows, n_rows), 0)
    t_col = lax.broadcasted_iota(jnp.int32, (n_rows, n_rows), 1)
    tri = ((t_row // c_len == t_col // c_len) & (t_row >= t_col)).astype(BF16)
    g_all = g_ref[...]
    g_hi = g_all.astype(BF16)
    rest = g_all - g_hi.astype(F32)
    g_mid = rest.astype(BF16)
    g_lo = (rest - g_mid.astype(F32)).astype(BF16)
    b_all = (jnp.dot(tri, g_hi, preferred_element_type=F32)
             + jnp.dot(tri, g_mid, preferred_element_type=F32)
             + jnp.dot(tri, g_lo, preferred_element_type=F32))

    def prepare(c):
        rows = slice(c * c_len, (c + 1) * c_len)
        g = g_all[rows]
        b = b_all[rows]
        b_last = b[c_len - 1:c_len, :]
        b_mid = b[c_len // 2 - 1:c_len // 2, :]
        safe = jnp.maximum(jnp.max(-b_mid), jnp.max(b_mid - b_last)) < SAFE_DECAY
        return rows, g, b, b_last, b_mid, safe

    def factored_scores(q, kk, b, b_mid):
        qb = (q * jnp.exp2(b - b_mid)).astype(BF16)
        kh = (kk * jnp.exp2(b_mid - b)).astype(BF16)

        def head(h):
            sl = slice(h * HEAD_W, (h + 1) * HEAD_W)
            s = lax.dot_general(qb[:, sl], kh[:, sl], _NT, preferred_element_type=F32)
            return jnp.where(causal, s, 0.0)
        return head

    def exact_scores(q, kk, b):
        def head(h):
            sl = slice(h * HEAD_W, (h + 1) * HEAD_W)
            qh, bh, kh = q[:, sl], b[:, sl], kk[:, sl]

            def col_body(s_idx, scores):
                pick = row_col == s_idx
                b_row = jnp.sum(jnp.where(pick, bh, 0.0), axis=0, keepdims=True)
                k_row = jnp.sum(jnp.where(pick, kh, 0.0), axis=0, keepdims=True)
                e = jnp.exp2(jnp.minimum(bh - b_row, 0.0))
                col = jnp.sum(qh * k_row * e, axis=1, keepdims=True)
                col = jnp.where(row_col >= s_idx, col, 0.0)
                return jnp.where(cols_i == s_idx, col, scores)

            return lax.fori_loop(0, c_len, col_body, jnp.zeros((c_len, c_len), F32))
        return head

    def finish(rows, q, kk, b, b_last, scores_of_head, state):
        qb = (q * jnp.exp2(b)).astype(BF16)
        kdb = (kk * jnp.exp2(b_last - b)).astype(BF16)
        decay_last = jnp.exp2(b_last)
        lanes = [slice(h * HEAD_W, (h + 1) * HEAD_W) for h in range(HEADS)]
        vs = [ri_ref[rows, sl] for sl in lanes]
        sc = [scores_of_head(h).astype(BF16) for h in range(HEADS)]
        upd = [lax.dot_general(vs[h], kdb[:, lanes[h]], _TN, preferred_element_type=F32)
               for h in range(HEADS)]
        intra = [jnp.dot(sc[h], vs[h], preferred_element_type=F32) for h in range(HEADS)]
        inter = [lax.dot_general(qb[:, lanes[h]], state[h].astype(BF16), _NT,
                                 preferred_element_type=F32) for h in range(HEADS)]
        for h, sl in enumerate(lanes):
            state[h] = state[h] * decay_last[:, sl] + upd[h]
            o = intra[h] + inter[h]
            o_ref[rows, sl] = (_rms(o) * gate_ref[rows, sl].astype(F32)).astype(o_ref.dtype)

    prepared = [prepare(c) for c in range(n_chunks)]
    all_safe = functools.reduce(jnp.logical_and, [p[5] for p in prepared])

    state = [st_scr[h] for h in range(HEADS)]
    for rows, g, b, b_last, b_mid, _ in prepared:
        q = rq_ref[rows, :].astype(F32)
        kk = 1.0 - jnp.exp2(g)
        finish(rows, q, kk, b, b_last, factored_scores(q, kk, b, b_mid), state)

    @pl.when(all_safe)
    def _():
        for h in range(HEADS):
            st_scr[h] = state[h]

    @pl.when(jnp.logical_not(all_safe))
    def _():
        redo_state = [st_scr[h] for h in range(HEADS)]
        for rows, g, b, b_last, b_mid, safe in prepared:
            q = rq_ref[rows, :].astype(F32)
            kk = 1.0 - jnp.exp2(g)

            @pl.when(safe)
            def _():
                head = factored_scores(q, kk, b, b_mid)
                for h in range(HEADS):
                    sc_scr[h] = head(h)

            @pl.when(jnp.logical_not(safe))
            def _():
                head = exact_scores(q, kk, b)
                for h in range(HEADS):
                    sc_scr[h] = head(h)

            finish(rows, q, kk, b, b_last, lambda h: sc_scr[h], redo_state)
        for h in range(HEADS):
            st_scr[h] = redo_state[h]


def _hgrn(rq, g, ri, gate, *, batch, seq, tt):
    t = rq.shape[0]
    tiles = seq // tt
    row = lambda bi, ti: (bi * tiles + ti, 0)
    kernel = functools.partial(_hgrn_kernel, n_chunks=tt // REC_CHUNK)
    return pl.pallas_call(
        kernel,
        grid=(batch, tiles),
        in_specs=[pl.BlockSpec((tt, D_MODEL), row)] * 4,
        out_specs=pl.BlockSpec((tt, D_MODEL), row),
        out_shape=jax.ShapeDtypeStruct((t, D_MODEL), BF16),
        scratch_shapes=[pltpu.VMEM((HEADS, HEAD_W, HEAD_W), F32),
                        pltpu.VMEM((HEADS, REC_CHUNK, REC_CHUNK), F32)],
        compiler_params=pltpu.CompilerParams(
            dimension_semantics=("arbitrary", "arbitrary"), vmem_limit_bytes=VMEM_LIMIT),
        name="hgrn",
    )(rq, g, ri, gate)


def _merge_kernel(oa_ref, or_ref, ga_ref, gr_ref, x_ref, ada_ref, wa_ref, wr_ref, wo_ref, o_ref):
    ya = lax.dot_general(oa_ref[0], wa_ref[...], _TN, preferred_element_type=F32)
    yr = jnp.dot(or_ref[...], wr_ref[...], preferred_element_type=F32)
    y = ga_ref[...].astype(F32) * ya + gr_ref[...].astype(F32) * yr
    upd = jnp.dot(y.astype(BF16), wo_ref[...], preferred_element_type=F32)
    o_ref[...] = x_ref[...] + ada_ref[0][2:3] * upd


def _merge(oa, orr, ga, gr, x2d, ada, wa, wr, wo, *, seq, tm):
    t = x2d.shape[0]
    tiles_per_batch = seq // tm
    row = lambda i: (i, 0)
    wspec = pl.BlockSpec((D_MODEL, D_MODEL), lambda i: (0, 0))
    return pl.pallas_call(
        _merge_kernel,
        grid=(t // tm,),
        in_specs=[pl.BlockSpec((1, D_MODEL, tm),
                               lambda i: (i // tiles_per_batch, 0, i % tiles_per_batch))]
                 + [pl.BlockSpec((tm, D_MODEL), row)] * 4
                 + [pl.BlockSpec((1, 8, D_MODEL), lambda i: (i // tiles_per_batch, 0, 0)),
                    wspec, wspec, wspec],
        out_specs=pl.BlockSpec((tm, D_MODEL), row),
        out_shape=jax.ShapeDtypeStruct((t, D_MODEL), F32),
        compiler_params=pltpu.CompilerParams(
            dimension_semantics=("arbitrary",), vmem_limit_bytes=VMEM_LIMIT),
        name="merge",
    )(oa, orr, ga, gr, x2d, ada, wa, wr, wo)


def _mlp_kernel(x_ref, ada_ref, nw_ref, nf_ref, w1_ref, w2_ref, o_ref):
    x = x_ref[...]
    ada = ada_ref[0]
    h = (_rms(x) * nw_ref[...] * (1.0 + ada[4:5]) + ada[3:4]).astype(BF16)
    acc = jnp.zeros_like(x)
    for c in range(D_FF // D_MODEL):
        cols = slice(c * D_MODEL, (c + 1) * D_MODEL)
        u = jnp.dot(h, w1_ref[:, cols], preferred_element_type=F32)
        u = jnp.square(jnp.maximum(u, 0.0))
        acc = acc + jnp.dot(u.astype(BF16), w2_ref[cols, :], preferred_element_type=F32)
    o_ref[...] = _rms(x + ada[5:6] * acc) * nf_ref[...]


def _mlp(x1, ada, norm_w, norm_final, w1, w2, *, seq, tm):
    t = x1.shape[0]
    tiles_per_batch = seq // tm
    row = lambda i: (i, 0)
    const = lambda i: (0, 0)
    return pl.pallas_call(
        _mlp_kernel,
        grid=(t // tm,),
        in_specs=[pl.BlockSpec((tm, D_MODEL), row),
                  pl.BlockSpec((1, 8, D_MODEL), lambda i: (i // tiles_per_batch, 0, 0)),
                  pl.BlockSpec((1, D_MODEL), const),
                  pl.BlockSpec((1, D_MODEL), const),
                  pl.BlockSpec((D_MODEL, D_FF), const, pipeline_mode=pl.Buffered(1)),
                  pl.BlockSpec((D_FF, D_MODEL), const, pipeline_mode=pl.Buffered(1))],
        out_specs=pl.BlockSpec((tm, D_MODEL), row),
        out_shape=jax.ShapeDtypeStruct((t, D_MODEL), F32),
        compiler_params=pltpu.CompilerParams(
            dimension_semantics=("arbitrary",), vmem_limit_bytes=VMEM_LIMIT),
        name="mlp",
    )(x1, ada, norm_w, norm_final, w1, w2)


def _rope_inv_freq_lanes():
    inv_freq = ROPE_THETA ** (-jnp.arange(0, ROPE_DIM, 2, dtype=F32) / ROPE_DIM)
    lane = jnp.arange(HEAD_W) % QK_DIM
    table = jnp.where(lane < ROPE_DIM, inv_freq[lane % (ROPE_DIM // 2)], 0.0)
    return table.reshape(1, HEAD_W).astype(F32)


def kernel(x, c, positions, w_ada, b_ada, norm_mix, w_in, lam_q1, lam_k1, lam_q2, lam_k2, subln_w, lb_logits, rec_norm_w, w_proj_att, w_proj_rec, w_out, norm_mlp, w_mlp_in, w_mlp_out, norm_final):
    batch, seq, d = x.shape
    assert d == D_MODEL and w_ada.shape[0] == 1, "single-layer, d_model=1024 only"
    assert batch <= 8
    t = batch * seq
    tm = min(512, seq)
    tq = min(512, seq)
    tt = min(256, seq)
    assert seq % tm == 0 and seq % tq == 0 and seq % tt == 0 and tt % REC_CHUNK == 0

    x2d = x.reshape(t, D_MODEL)
    c_pad = jnp.zeros((8, D_MODEL), F32).at[:batch].set(c)
    ada = _ada(c_pad, w_ada[0], b_ada[0].reshape(1, -1))[:batch]
    ada = jnp.pad(ada.reshape(batch, N_ADA, D_MODEL), ((0, 0), (0, 8 - N_ADA), (0, 0)))

    q, k, v, rq, g, ri, gate, ga, gr = _inproj(
        x2d, ada, norm_mix[0].reshape(1, -1), w_in[0].astype(BF16),
        positions.reshape(t, 1), _rope_inv_freq_lanes(), lb_logits,
        jnp.tile(rec_norm_w.reshape(1, HEAD_W), (1, HEADS)), seq=seq, tm=tm)

    as3 = lambda a: a.reshape(batch, seq, D_MODEL)
    o_a = _attn(as3(q), as3(k), as3(v), lam_q1, lam_k1, lam_q2, lam_k2,
                subln_w.reshape(HEAD_W, 1), tq=tq)
    o_r = _hgrn(rq, g, ri, gate, batch=batch, seq=seq, tt=tt)

    x1 = _merge(o_a, o_r, ga, gr, x2d, ada,
                w_proj_att[0].astype(BF16), w_proj_rec[0].astype(BF16), w_out[0].astype(BF16),
                seq=seq, tm=tm)
    out = _mlp(x1, ada, norm_mlp[0].reshape(1, -1), norm_final.reshape(1, -1),
               w_mlp_in[0].astype(BF16), w_mlp_out[0].astype(BF16), seq=seq, tm=tm)
    return out.reshape(batch, seq, D_MODEL)
```

```python
import functools
import math

import jax
import jax.numpy as jnp
from jax import lax
from jax.experimental import pallas as pl
from jax.experimental.pallas import tpu as pltpu

F32 = jnp.float32
BF16 = jnp.bfloat16

D_MODEL = 1024
HEADS = 8
HEAD_W = 128
QK_DIM = 64
ROPE_DIM = 16
ROPE_THETA = 500000.0
D_FF = 4 * D_MODEL
N_ADA = 6
N_SECTIONS = 9
REC_CHUNK = 64
EPS = 1e-6
LAM_INIT = 0.8 - 0.6 * math.exp(-0.3 * 0)
LOG2_E = 1.0 / math.log(2.0)
SAFE_DECAY = 115.0
MAX_RISE = 64.0
SUB_KEYS = 256
STREAM_TILES = 4
QK_AHEAD = 2
HEADS_PER_STEP = 4
ROW_GROUPS = 2
VMEM_LIMIT = 56 * 1024 * 1024

_NT = (((1,), (1,)), ((), ()))
_TN = (((0,), (0,)), ((), ()))


def _rms(x):
    return x * lax.rsqrt(jnp.mean(x * x, axis=-1, keepdims=True) + EPS)


def _ada_kernel(c_ref, w_ref, b_ref, o_ref):
    c = c_ref[...]
    cond = c * jax.nn.sigmoid(c)
    o_ref[...] = jnp.dot(cond, w_ref[...], preferred_element_type=F32,
                         precision=lax.Precision.HIGHEST) + b_ref[...]


def _ada(c_pad, w_ada, b_ada):
    n = w_ada.shape[1]
    return pl.pallas_call(
        _ada_kernel,
        grid=(n // D_MODEL,),
        in_specs=[pl.BlockSpec((8, D_MODEL), lambda j: (0, 0)),
                  pl.BlockSpec((D_MODEL, D_MODEL), lambda j: (0, j)),
                  pl.BlockSpec((1, D_MODEL), lambda j: (0, j))],
        out_specs=pl.BlockSpec((8, D_MODEL), lambda j: (0, j)),
        out_shape=jax.ShapeDtypeStruct((8, n), F32),
        name="ada",
    )(c_pad, w_ada, b_ada)


def _inproj_kernel(x_ref, ada_ref, nw_ref, w_ref, pos_ref, invf_ref, lb_ref, rnw_ref,
                   q_ref, k_ref, v_ref, rq_ref, g_ref, ri_ref, gate_ref, ga_ref, gr_ref):
    ada = ada_ref[0]
    rows = x_ref.shape[0] // ROW_GROUPS
    hs = [(_rms(x_ref[r * rows:(r + 1) * rows, :]) * nw_ref[...] * (1.0 + ada[1:2]) + ada[0:1]).astype(BF16)
          for r in range(ROW_GROUPS)]

    ang = pos_ref[...].astype(F32) * invf_ref[...]
    lane = lax.broadcasted_iota(jnp.int32, (1, HEAD_W), 1) % QK_DIM
    cos_a = jnp.cos(ang)
    sin_a = jnp.sin(ang)
    cs = jnp.where(lane < ROPE_DIM, cos_a, 1.0)
    s1 = jnp.where(lane < ROPE_DIM // 2, -sin_a, 0.0)
    s2 = jnp.where((lane >= ROPE_DIM // 2) & (lane < ROPE_DIM), sin_a, 0.0)

    def section(j):
        w = w_ref[:, j * D_MODEL:(j + 1) * D_MODEL]
        return jnp.concatenate([jnp.dot(hh, w, preferred_element_type=F32) for hh in hs], axis=0)

    def rope_store(ref, acc, scale):
        c_, s1_, s2_ = cs * scale, s1 * scale, s2 * scale
        for hh in range(HEADS):
            t = acc[:, hh * HEAD_W:(hh + 1) * HEAD_W]
            r = (t * c_ + pltpu.roll(t, HEAD_W - ROPE_DIM // 2, 1) * s1_
                 + pltpu.roll(t, ROPE_DIM // 2, 1) * s2_)
            ref[:, hh * HEAD_W:(hh + 1) * HEAD_W] = r.astype(ref.dtype)

    v_ref[...] = section(2).astype(v_ref.dtype)
    rope_store(q_ref, section(0), QK_DIM ** -0.5 * LOG2_E)
    rq_ref[...] = section(3).astype(rq_ref.dtype)
    rope_store(k_ref, section(1), 1.0)
    ri_ref[...] = section(5).astype(ri_ref.dtype)

    l0 = lb_ref[0:1, :]
    l1 = lb_ref[1:2, :]
    mx = jnp.maximum(l0, l1)
    e0 = jnp.exp(l0 - mx)
    e1 = jnp.exp(l1 - mx)
    lb = e1 / (e0 + e1)
    g_ref[...] = jnp.log(lb + (1.0 - lb) * jax.nn.sigmoid(section(4))) * LOG2_E

    rg = section(6)
    gate_ref[...] = (rnw_ref[...] * (rg * jax.nn.sigmoid(rg))).astype(gate_ref.dtype)
    ga_ref[...] = jax.nn.sigmoid(section(7)).astype(ga_ref.dtype)
    gr_ref[...] = jax.nn.sigmoid(section(8)).astype(gr_ref.dtype)


def _inproj(x2d, ada, norm_w, w_in, pos, invf, lb_logits, rec_norm_row, *, seq, tm):
    t = x2d.shape[0]
    tiles_per_batch = seq // tm
    row = lambda i: (i, 0)
    const2 = lambda i: (0, 0)
    out_dtypes = (BF16, BF16, BF16, BF16, F32, BF16, BF16, BF16, BF16)
    return pl.pallas_call(
        _inproj_kernel,
        grid=(t // tm,),
        in_specs=[pl.BlockSpec((tm, D_MODEL), row),
                  pl.BlockSpec((1, 8, D_MODEL), lambda i: (i // tiles_per_batch, 0, 0)),
                  pl.BlockSpec((1, D_MODEL), const2),
                  pl.BlockSpec((D_MODEL, N_SECTIONS * D_MODEL), const2, pipeline_mode=pl.Buffered(1)),
                  pl.BlockSpec((tm, 1), row),
                  pl.BlockSpec((1, HEAD_W), const2),
                  pl.BlockSpec((2, D_MODEL), const2),
                  pl.BlockSpec((1, D_MODEL), const2)],
        out_specs=[pl.BlockSpec((tm, D_MODEL), row) for _ in out_dtypes],
        out_shape=[jax.ShapeDtypeStruct((t, D_MODEL), dt) for dt in out_dtypes],
        compiler_params=pltpu.CompilerParams(
            dimension_semantics=("arbitrary",), vmem_limit_bytes=VMEM_LIMIT),
        name="inproj",
    )(x2d, ada, norm_w, w_in, pos, invf, lb_logits, rec_norm_row)


def _attn_kernel(lq1_ref, lk1_ref, lq2_ref, lk2_ref, sw_ref, q_ref, k_ref, v_ref, o_ref,
                 vt_scr, acc, m, l, dev, *, tq):
    qi = pl.program_id(2)
    n_tiles = pl.num_programs(2)
    n_streams = 2 * HEADS_PER_STEP

    @pl.when(qi == 0)
    def _():
        def tr(j, carry):
            start = pl.multiple_of(j * tq, tq)
            vt_scr[:, pl.ds(start, tq)] = v_ref[0, pl.ds(start, tq), :].T
            return carry
        lax.fori_loop(0, n_tiles, tr, 0)

    def head_lanes(s):
        return slice((s // 2) * HEAD_W, (s // 2 + 1) * HEAD_W)

    q = q_ref[0]
    lane = lax.broadcasted_iota(jnp.int32, (1, HEAD_W), 1)
    q_streams = []
    for s in range(n_streams):
        qh = q[:, head_lanes(s)]
        keep = (lane < QK_DIM) if s % 2 == 0 else (lane >= QK_DIM)
        q_streams.append(jnp.where(keep, qh, jnp.zeros_like(qh)))

    def init():
        acc[...] = jnp.zeros_like(acc)
        m[...] = jnp.full_like(m, -jnp.inf)
        l[...] = jnp.zeros_like(l)
        dev[...] = jnp.zeros_like(dev)

    def tiles(kj):
        start = pl.multiple_of(kj * tq, tq)
        return k_ref[0, pl.ds(start, tq), :], vt_scr[:, pl.ds(start, tq)]

    def exact_step(kj):
        k, vt = tiles(kj)
        for s in range(n_streams):
            hl = head_lanes(s)
            st = lax.dot_general(k[:, hl], q_streams[s], _NT, preferred_element_type=F32)
            kr = lax.broadcasted_iota(jnp.int32, st.shape, 0) + (kj - qi) * tq
            qc = lax.broadcasted_iota(jnp.int32, st.shape, 1)
            st = jnp.where(kr <= qc, st, -jnp.inf)
            m_prev = m[s]
            m_new = jnp.maximum(m_prev, jnp.max(st, axis=0, keepdims=True))
            alpha = jnp.exp2(m_prev - m_new)
            pt = jnp.exp2(st - m_new)
            l[s] = alpha * l[s] + jnp.sum(pt, axis=0, keepdims=True)
            acc[s] = alpha * acc[s] + jnp.dot(vt[hl, :], pt.astype(BF16), preferred_element_type=F32)
            m[s] = m_new

    def stream_tiles(kj0, n, after_first_scores=None):
        kv = [tiles(kj0 + t) for t in range(n)]
        n_sub = tq // SUB_KEYS
        chains = [(t, s, h) for t in range(n) for s in range(n_streams) for h in range(n_sub)]

        def qk(c):
            t, s, h = c
            ks = slice(h * SUB_KEYS, (h + 1) * SUB_KEYS)
            return lax.dot_general(kv[t][0][ks, head_lanes(s)], q_streams[s], _NT,
                                   preferred_element_type=F32)

        scores = {i: qk(c) for i, c in enumerate(chains[:QK_AHEAD])}
        if after_first_scores is not None:
            after_first_scores()
        part = {}
        for i, (t, s, h) in enumerate(chains):
            st = scores.pop(i)
            m_prev = m[s]
            pt = jnp.exp2(st - m_prev)
            mx = jnp.max(st, axis=0, keepdims=True)
            sm = jnp.sum(pt, axis=0, keepdims=True)
            if i + QK_AHEAD < len(chains):
                scores[i + QK_AHEAD] = qk(chains[i + QK_AHEAD])
            ks = slice(h * SUB_KEYS, (h + 1) * SUB_KEYS)
            pp = jnp.dot(kv[t][1][head_lanes(s), ks], pt.astype(BF16), preferred_element_type=F32)
            if h == 0:
                part[s] = (mx, sm, pp)
            else:
                mx0, sm0, pp0 = part[s]
                part[s] = (jnp.maximum(mx0, mx), sm0 + sm, pp0 + pp)
            if h == n_sub - 1:
                t_max, l_add, pv = part.pop(s)
                m_new = jnp.maximum(m_prev, t_max)
                alpha = jnp.exp2(m_prev - m_new)
                dev[s] = jnp.maximum(dev[s], t_max - m_prev)
                l[s] = alpha * (l[s] + l_add)
                acc[s] = alpha * (acc[s] + pv)
                m[s] = m_new

    def diag_step(then_stream):
        k, vt = tiles(qi)
        half = tq // 2
        problems = ((slice(0, half), slice(0, half)), (slice(0, tq), slice(half, tq)))
        sts = {(s, pi): lax.dot_general(k[ks, head_lanes(s)], q_streams[s][qs], _NT,
                                        preferred_element_type=F32)
               for s in range(n_streams) for pi, (ks, qs) in enumerate(problems)}
        dev[...] = jnp.zeros_like(dev)
        pending = []
        for s in range(n_streams):
            for pi, (ks, qs) in enumerate(problems):
                st = sts[s, pi]
                kr = lax.broadcasted_iota(jnp.int32, st.shape, 0)
                qc = lax.broadcasted_iota(jnp.int32, st.shape, 1) + qs.start
                st = jnp.where(kr <= qc, st, -jnp.inf)
                m_new = jnp.max(st, axis=0, keepdims=True)
                pt = jnp.exp2(st - m_new)
                l[s, :, qs] = jnp.sum(pt, axis=0, keepdims=True)
                m[s, :, qs] = m_new
                pending.append((s, ks, qs, pt.astype(BF16)))

        def value_matmuls():
            for s, ks, qs, ptb in pending:
                acc[s, :, qs] = jnp.dot(vt[head_lanes(s), ks], ptb, preferred_element_type=F32)

        if then_stream:
            stream_tiles(0, then_stream, after_first_scores=value_matmuls)
        else:
            value_matmuls()

    first = qi % STREAM_TILES
    for r in range(STREAM_TILES):
        @pl.when(first == r)
        def _(r=r):
            diag_step(r)

    def body(t, carry):
        stream_tiles(first + STREAM_TILES * t, STREAM_TILES)
        return carry

    lax.fori_loop(0, qi // STREAM_TILES, body, 0)

    lam = (jnp.exp(jnp.sum(lq1_ref[...] * lk1_ref[...], axis=-1, keepdims=True))
           - jnp.exp(jnp.sum(lq2_ref[...] * lk2_ref[...], axis=-1, keepdims=True)) + LAM_INIT)

    def write_output():
        for hh in range(HEADS_PER_STEP):
            s1, s2 = 2 * hh, 2 * hh + 1
            ot = acc[s1] / l[s1] - lam * (acc[s2] / l[s2])
            inv = lax.rsqrt(jnp.mean(ot * ot, axis=0, keepdims=True) + EPS)
            ot = ot * inv * sw_ref[...] * (1.0 - LAM_INIT)
            o_ref[0, hh * HEAD_W:(hh + 1) * HEAD_W, :] = ot.astype(o_ref.dtype)

    write_output()

    rise = jnp.max(dev[...])

    @pl.when(jnp.logical_not(rise <= MAX_RISE))
    def _():
        init()

        def redo(kj, carry):
            exact_step(kj)
            return carry
        lax.fori_loop(0, qi + 1, redo, 0)
        write_output()


def _attn(q3, k3, v3, lq1, lk1, lq2, lk2, subln_col, *, tq):
    b, s, _ = q3.shape
    vec = lambda bi, h, qi: (0, 0)
    width = HEADS_PER_STEP * HEAD_W
    n_streams = 2 * HEADS_PER_STEP
    kernel = functools.partial(_attn_kernel, tq=tq)
    return pl.pallas_call(
        kernel,
        grid=(b, HEADS // HEADS_PER_STEP, s // tq),
        in_specs=[pl.BlockSpec((1, QK_DIM), vec)] * 4
                 + [pl.BlockSpec((HEAD_W, 1), vec),
                    pl.BlockSpec((1, tq, width), lambda bi, h, qi: (bi, qi, h)),
                    pl.BlockSpec((1, s, width), lambda bi, h, qi: (bi, 0, h)),
                    pl.BlockSpec((1, s, width), lambda bi, h, qi: (bi, 0, h))],
        out_specs=pl.BlockSpec((1, width, tq), lambda bi, h, qi: (bi, h, qi)),
        out_shape=jax.ShapeDtypeStruct((b, D_MODEL, s), BF16),
        scratch_shapes=[pltpu.VMEM((width, s), BF16),
                        pltpu.VMEM((n_streams, HEAD_W, tq), F32),
                        pltpu.VMEM((n_streams, 1, tq), F32),
                        pltpu.VMEM((n_streams, 1, tq), F32),
                        pltpu.VMEM((n_streams, 1, tq), F32)],
        compiler_params=pltpu.CompilerParams(
            dimension_semantics=("arbitrary", "arbitrary", "arbitrary"), vmem_limit_bytes=VMEM_LIMIT),
        name="attn",
    )(lq1, lk1, lq2, lk2, subln_col, q3, k3, v3)


def _hgrn_kernel(rq_ref, g_ref, ri_ref, gate_ref, o_ref, st_scr, sc_scr, *, n_chunks):
    c_len = REC_CHUNK

    @pl.when(pl.program_id(1) == 0)
    def _():
        st_scr[...] = jnp.zeros_like(st_scr)

    rows_i = lax.broadcasted_iota(jnp.int32, (c_len, c_len), 0)
    cols_i = lax.broadcasted_iota(jnp.int32, (c_len, c_len), 1)
    causal = rows_i >= cols_i
    row_col = lax.broadcasted_iota(jnp.int32, (c_len, 1), 0)

    n_rows = n_chunks * c_len
    t_row = lax.broadcasted_iota(jnp.int32, (n_rows, n_rows), 0)
    t_col = lax.broadcasted_iota(jnp.int32, (n_rows, n_rows), 1)
    tri = ((t_row // c_len == t_col // c_len) & (t_row >= t_col)).astype(BF16)
    g_all = g_ref[...]
    g_hi = g_all.astype(BF16)
    rest = g_all - g_hi.astype(F32)
    g_mid = rest.astype(BF16)
    g_lo = (rest - g_mid.astype(F32)).astype(BF16)
    b_all = (jnp.dot(tri, g_hi, preferred_element_type=F32)
             + jnp.dot(tri, g_mid, preferred_element_type=F32)
             + jnp.dot(tri, g_lo, preferred_element_type=F32))

    def prepare(c):
        rows = slice(c * c_len, (c + 1) * c_len)
        g = g_all[rows]
        b = b_all[rows]
        b_last = b[c_len - 1:c_len, :]
        b_mid = b[c_len // 2 - 1:c_len // 2, :]
        safe = jnp.maximum(jnp.max(-b_mid), jnp.max(b_mid - b_last)) < SAFE_DECAY
        return rows, g, b, b_last, b_mid, safe

    def factored_scores(q, kk, b, b_mid):
        qb = (q * jnp.exp2(b - b_mid)).astype(BF16)
        kh = (kk * jnp.exp2(b_mid - b)).astype(BF16)

        def head(h):
            sl = slice(h * HEAD_W, (h + 1) * HEAD_W)
            s = lax.dot_general(qb[:, sl], kh[:, sl], _NT, preferred_element_type=F32)
            return jnp.where(causal, s, 0.0)
        return head

    def exact_scores(q, kk, b):
        def head(h):
            sl = slice(h * HEAD_W, (h + 1) * HEAD_W)
            qh, bh, kh = q[:, sl], b[:, sl], kk[:, sl]

            def col_body(s_idx, scores):
                pick = row_col == s_idx
                b_row = jnp.sum(jnp.where(pick, bh, 0.0), axis=0, keepdims=True)
                k_row = jnp.sum(jnp.where(pick, kh, 0.0), axis=0, keepdims=True)
                e = jnp.exp2(jnp.minimum(bh - b_row, 0.0))
                col = jnp.sum(qh * k_row * e, axis=1, keepdims=True)
                col = jnp.where(row_col >= s_idx, col, 0.0)
                return jnp.where(cols_i == s_idx, col, scores)

            return lax.fori_loop(0, c_len, col_body, jnp.zeros((c_len, c_len), F32))
        return head

    def finish(rows, q, kk, b, b_last, scores_of_head, state):
        qb = (q * jnp.exp2(b)).astype(BF16)
        kdb = (kk * jnp.exp2(b_last - b)).astype(BF16)
        decay_last = jnp.exp2(b_last)
        lanes = [slice(h * HEAD_W, (h + 1) * HEAD_W) for h in range(HEADS)]
        vs = [ri_ref[rows, sl] for sl in lanes]
        sc = [scores_of_head(h).astype(BF16) for h in range(HEADS)]
        upd = [lax.dot_general(vs[h], kdb[:, lanes[h]], _TN, preferred_element_type=F32)
               for h in range(HEADS)]
        intra = [jnp.dot(sc[h], vs[h], preferred_element_type=F32) for h in range(HEADS)]
        inter = [lax.dot_general(qb[:, lanes[h]], state[h].astype(BF16), _NT,
                                 preferred_element_type=F32) for h in range(HEADS)]
        for h, sl in enumerate(lanes):
            state[h] = state[h] * decay_last[:, sl] + upd[h]
            o = intra[h] + inter[h]
            o_ref[rows, sl] = (_rms(o) * gate_ref[rows, sl].astype(F32)).astype(o_ref.dtype)

    prepared = [prepare(c) for c in range(n_chunks)]
    all_safe = functools.reduce(jnp.logical_and, [p[5] for p in prepared])

    state = [st_scr[h] for h in range(HEADS)]
    for rows, g, b, b_last, b_mid, _ in prepared:
        q = rq_ref[rows, :].astype(F32)
        kk = 1.0 - jnp.exp2(g)
        finish(rows, q, kk, b, b_last, factored_scores(q, kk, b, b_mid), state)

    @pl.when(all_safe)
    def _():
        for h in range(HEADS):
            st_scr[h] = state[h]

    @pl.when(jnp.logical_not(all_safe))
    def _():
        redo_state = [st_scr[h] for h in range(HEADS)]
        for rows, g, b, b_last, b_mid, safe in prepared:
            q = rq_ref[rows, :].astype(F32)
            kk = 1.0 - jnp.exp2(g)

            @pl.when(safe)
            def _():
                head = factored_scores(q, kk, b, b_mid)
                for h in range(HEADS):
                    sc_scr[h] = head(h)

            @pl.when(jnp.logical_not(safe))
            def _():
                head = exact_scores(q, kk, b)
                for h in range(HEADS):
                    sc_scr[h] = head(h)

            finish(rows, q, kk, b, b_last, lambda h: sc_scr[h], redo_state)
        for h in range(HEADS):
            st_scr[h] = redo_state[h]


def _hgrn(rq, g, ri, gate, *, batch, seq, tt):
    t = rq.shape[0]
    tiles = seq // tt
    row = lambda bi, ti: (bi * tiles + ti, 0)
    kernel = functools.partial(_hgrn_kernel, n_chunks=tt // REC_CHUNK)
    return pl.pallas_call(
        kernel,
        grid=(batch, tiles),
        in_specs=[pl.BlockSpec((tt, D_MODEL), row)] * 4,
        out_specs=pl.BlockSpec((tt, D_MODEL), row),
        out_shape=jax.ShapeDtypeStruct((t, D_MODEL), BF16),
        scratch_shapes=[pltpu.VMEM((HEADS, HEAD_W, HEAD_W), F32),
                        pltpu.VMEM((HEADS, REC_CHUNK, REC_CHUNK), F32)],
        compiler_params=pltpu.CompilerParams(
            dimension_semantics=("arbitrary", "arbitrary"), vmem_limit_bytes=VMEM_LIMIT),
        name="hgrn",
    )(rq, g, ri, gate)


def _merge_kernel(oa_ref, or_ref, ga_ref, gr_ref, x_ref, ada_ref, wa_ref, wr_ref, wo_ref, o_ref):
    ya = lax.dot_general(oa_ref[0], wa_ref[...], _TN, preferred_element_type=F32)
    yr = jnp.dot(or_ref[...], wr_ref[...], preferred_element_type=F32)
    y = ga_ref[...].astype(F32) * ya + gr_ref[...].astype(F32) * yr
    upd = jnp.dot(y.astype(BF16), wo_ref[...], preferred_element_type=F32)
    o_ref[...] = x_ref[...] + ada_ref[0][2:3] * upd


def _merge(oa, orr, ga, gr, x2d, ada, wa, wr, wo, *, seq, tm):
    t = x2d.shape[0]
    tiles_per_batch = seq // tm
    row = lambda i: (i, 0)
    wspec = pl.BlockSpec((D_MODEL, D_MODEL), lambda i: (0, 0))
    return pl.pallas_call(
        _merge_kernel,
        grid=(t // tm,),
        in_specs=[pl.BlockSpec((1, D_MODEL, tm),
                               lambda i: (i // tiles_per_batch, 0, i % tiles_per_batch))]
                 + [pl.BlockSpec((tm, D_MODEL), row)] * 4
                 + [pl.BlockSpec((1, 8, D_MODEL), lambda i: (i // tiles_per_batch, 0, 0)),
                    wspec, wspec, wspec],
        out_specs=pl.BlockSpec((tm, D_MODEL), row),
        out_shape=jax.ShapeDtypeStruct((t, D_MODEL), F32),
        compiler_params=pltpu.CompilerParams(
            dimension_semantics=("arbitrary",), vmem_limit_bytes=VMEM_LIMIT),
        name="merge",
    )(oa, orr, ga, gr, x2d, ada, wa, wr, wo)


def _mlp_kernel(x_ref, ada_ref, nw_ref, nf_ref, w1_ref, w2_ref, o_ref):
    x = x_ref[...]
    ada = ada_ref[0]
    h = (_rms(x) * nw_ref[...] * (1.0 + ada[4:5]) + ada[3:4]).astype(BF16)
    acc = jnp.zeros_like(x)
    for c in range(D_FF // D_MODEL):
        cols = slice(c * D_MODEL, (c + 1) * D_MODEL)
        u = jnp.dot(h, w1_ref[:, cols], preferred_element_type=F32)
        u = jnp.square(jnp.maximum(u, 0.0))
        acc = acc + jnp.dot(u.astype(BF16), w2_ref[cols, :], preferred_element_type=F32)
    o_ref[...] = _rms(x + ada[5:6] * acc) * nf_ref[...]


def _mlp(x1, ada, norm_w, norm_final, w1, w2, *, seq, tm):
    t = x1.shape[0]
    tiles_per_batch = seq // tm
    row = lambda i: (i, 0)
    const = lambda i: (0, 0)
    return pl.pallas_call(
        _mlp_kernel,
        grid=(t // tm,),
        in_specs=[pl.BlockSpec((tm, D_MODEL), row),
                  pl.BlockSpec((1, 8, D_MODEL), lambda i: (i // tiles_per_batch, 0, 0)),
                  pl.BlockSpec((1, D_MODEL), const),
                  pl.BlockSpec((1, D_MODEL), const),
                  pl.BlockSpec((D_MODEL, D_FF), const, pipeline_mode=pl.Buffered(1)),
                  pl.BlockSpec((D_FF, D_MODEL), const, pipeline_mode=pl.Buffered(1))],
        out_specs=pl.BlockSpec((tm, D_MODEL), row),
        out_shape=jax.ShapeDtypeStruct((t, D_MODEL), F32),
        compiler_params=pltpu.CompilerParams(
            dimension_semantics=("arbitrary",), vmem_limit_bytes=VMEM_LIMIT),
        name="mlp",
    )(x1, ada, norm_w, norm_final, w1, w2)


def _rope_inv_freq_lanes():
    inv_freq = ROPE_THETA ** (-jnp.arange(0, ROPE_DIM, 2, dtype=F32) / ROPE_DIM)
    lane = jnp.arange(HEAD_W) % QK_DIM
    table = jnp.where(lane < ROPE_DIM, inv_freq[lane % (ROPE_DIM // 2)], 0.0)
    return table.reshape(1, HEAD_W).astype(F32)


def kernel(x, c, positions, w_ada, b_ada, norm_mix, w_in, lam_q1, lam_k1, lam_q2, lam_k2, subln_w, lb_logits, rec_norm_w, w_proj_att, w_proj_rec, w_out, norm_mlp, w_mlp_in, w_mlp_out, norm_final):
    batch, seq, d = x.shape
    assert d == D_MODEL and w_ada.shape[0] == 1, "single-layer, d_model=1024 only"
    assert batch <= 8
    t = batch * seq
    tm = min(512, seq)
    tq = min(512, seq)
    tt = min(256, seq)
    assert seq % tm == 0 and seq % tq == 0 and seq % tt == 0 and tt % REC_CHUNK == 0

    x2d = x.reshape(t, D_MODEL)
    c_pad = jnp.zeros((8, D_MODEL), F32).at[:batch].set(c)
    ada = _ada(c_pad, w_ada[0], b_ada[0].reshape(1, -1))[:batch]
    ada = jnp.pad(ada.reshape(batch, N_ADA, D_MODEL), ((0, 0), (0, 8 - N_ADA), (0, 0)))

    q, k, v, rq, g, ri, gate, ga, gr = _inproj(
        x2d, ada, norm_mix[0].reshape(1, -1), w_in[0].astype(BF16),
        positions.reshape(t, 1), _rope_inv_freq_lanes(), lb_logits,
        jnp.tile(rec_norm_w.reshape(1, HEAD_W), (1, HEADS)), seq=seq, tm=tm)

    as3 = lambda a: a.reshape(batch, seq, D_MODEL)
    o_a = _attn(as3(q), as3(k), as3(v), lam_q1, lam_k1, lam_q2, lam_k2,
                subln_w.reshape(HEAD_W, 1), tq=tq)
    o_r = _hgrn(rq, g, ri, gate, batch=batch, seq=seq, tt=tt)

    x1 = _merge(o_a, o_r, ga, gr, x2d, ada,
                w_proj_att[0].astype(BF16), w_proj_rec[0].astype(BF16), w_out[0].astype(BF16),
                seq=seq, tm=tm)
    out = _mlp(x1, ada, norm_mlp[0].reshape(1, -1), norm_final.reshape(1, -1),
               w_mlp_in[0].astype(BF16), w_mlp_out[0].astype(BF16), seq=seq, tm=tm)
    return out.reshape(batch, seq, D_MODEL)
```

```python
import functools
import math

import jax
import jax.numpy as jnp
from jax import lax
from jax.experimental import pallas as pl
from jax.experimental.pallas import tpu as pltpu

F32 = jnp.float32
BF16 = jnp.bfloat16

D_MODEL = 1024
HEADS = 8
HEAD_W = 128
QK_DIM = 64
ROPE_DIM = 16
ROPE_THETA = 500000.0
D_FF = 4 * D_MODEL
N_ADA = 6
N_SECTIONS = 9
REC_CHUNK = 64
EPS = 1e-6
LAM_INIT = 0.8 - 0.6 * math.exp(-0.3 * 0)
LOG2_E = 1.0 / math.log(2.0)
SAFE_DECAY = 115.0
MAX_RISE = 64.0
SUB_KEYS = 256
STREAM_TILES = 4
QK_AHEAD = 2
HEADS_PER_STEP = 4
ROW_GROUPS = 2
VMEM_LIMIT = 56 * 1024 * 1024

_NT = (((1,), (1,)), ((), ()))
_TN = (((0,), (0,)), ((), ()))


def _rms(x):
    return x * lax.rsqrt(jnp.mean(x * x, axis=-1, keepdims=True) + EPS)


def _ada_kernel(c_ref, w_ref, b_ref, o_ref):
    c = c_ref[...]
    cond = c * jax.nn.sigmoid(c)
    o_ref[...] = jnp.dot(cond, w_ref[...], preferred_element_type=F32,
                         precision=lax.Precision.HIGHEST) + b_ref[...]


def _ada(c_pad, w_ada, b_ada):
    n = w_ada.shape[1]
    return pl.pallas_call(
        _ada_kernel,
        grid=(n // D_MODEL,),
        in_specs=[pl.BlockSpec((8, D_MODEL), lambda j: (0, 0)),
                  pl.BlockSpec((D_MODEL, D_MODEL), lambda j: (0, j)),
                  pl.BlockSpec((1, D_MODEL), lambda j: (0, j))],
        out_specs=pl.BlockSpec((8, D_MODEL), lambda j: (0, j)),
        out_shape=jax.ShapeDtypeStruct((8, n), F32),
        name="ada",
    )(c_pad, w_ada, b_ada)


def _inproj_kernel(x_ref, ada_ref, nw_ref, w_ref, pos_ref, invf_ref, lb_ref, rnw_ref,
                   q_ref, k_ref, v_ref, rq_ref, g_ref, ri_ref, gate_ref, ga_ref, gr_ref):
    ada = ada_ref[0]
    rows = x_ref.shape[0] // ROW_GROUPS
    hs = [(_rms(x_ref[r * rows:(r + 1) * rows, :]) * nw_ref[...] * (1.0 + ada[1:2]) + ada[0:1]).astype(BF16)
          for r in range(ROW_GROUPS)]

    ang = pos_ref[...].astype(F32) * invf_ref[...]
    lane = lax.broadcasted_iota(jnp.int32, (1, HEAD_W), 1) % QK_DIM
    cos_a = jnp.cos(ang)
    sin_a = jnp.sin(ang)
    cs = jnp.where(lane < ROPE_DIM, cos_a, 1.0)
    s1 = jnp.where(lane < ROPE_DIM // 2, -sin_a, 0.0)
    s2 = jnp.where((lane >= ROPE_DIM // 2) & (lane < ROPE_DIM), sin_a, 0.0)

    def section(j):
        w = w_ref[:, j * D_MODEL:(j + 1) * D_MODEL]
        return jnp.concatenate([jnp.dot(hh, w, preferred_element_type=F32) for hh in hs], axis=0)

    def rope_store(ref, acc, scale):
        c_, s1_, s2_ = cs * scale, s1 * scale, s2 * scale
        for hh in range(HEADS):
            t = acc[:, hh * HEAD_W:(hh + 1) * HEAD_W]
            r = (t * c_ + pltpu.roll(t, HEAD_W - ROPE_DIM // 2, 1) * s1_
                 + pltpu.roll(t, ROPE_DIM // 2, 1) * s2_)
            ref[:, hh * HEAD_W:(hh + 1) * HEAD_W] = r.astype(ref.dtype)

    v_ref[...] = section(2).astype(v_ref.dtype)
    rope_store(q_ref, section(0), QK_DIM ** -0.5 * LOG2_E)
    rq_ref[...] = section(3).astype(rq_ref.dtype)
    rope_store(k_ref, section(1), 1.0)
    ri_ref[...] = section(5).astype(ri_ref.dtype)

    l0 = lb_ref[0:1, :]
    l1 = lb_ref[1:2, :]
    mx = jnp.maximum(l0, l1)
    e0 = jnp.exp(l0 - mx)
    e1 = jnp.exp(l1 - mx)
    lb = e1 / (e0 + e1)
    g_ref[...] = jnp.log(lb + (1.0 - lb) * jax.nn.sigmoid(section(4))) * LOG2_E

    rg = section(6)
    gate_ref[...] = (rnw_ref[...] * (rg * jax.nn.sigmoid(rg))).astype(gate_ref.dtype)
    ga_ref[...] = jax.nn.sigmoid(section(7)).astype(ga_ref.dtype)
    gr_ref[...] = jax.nn.sigmoid(section(8)).astype(gr_ref.dtype)


def _inproj(x2d, ada, norm_w, w_in, pos, invf, lb_logits, rec_norm_row, *, seq, tm):
    t = x2d.shape[0]
    tiles_per_batch = seq // tm
    row = lambda i: (i, 0)
    const2 = lambda i: (0, 0)
    out_dtypes = (BF16, BF16, BF16, BF16, F32, BF16, BF16, BF16, BF16)
    return pl.pallas_call(
        _inproj_kernel,
        grid=(t // tm,),
        in_specs=[pl.BlockSpec((tm, D_MODEL), row),
                  pl.BlockSpec((1, 8, D_MODEL), lambda i: (i // tiles_per_batch, 0, 0)),
                  pl.BlockSpec((1, D_MODEL), const2),
                  pl.BlockSpec((D_MODEL, N_SECTIONS * D_MODEL), const2, pipeline_mode=pl.Buffered(1)),
                  pl.BlockSpec((tm, 1), row),
                  pl.BlockSpec((1, HEAD_W), const2),
                  pl.BlockSpec((2, D_MODEL), const2),
                  pl.BlockSpec((1, D_MODEL), const2)],
        out_specs=[pl.BlockSpec((tm, D_MODEL), row) for _ in out_dtypes],
        out_shape=[jax.ShapeDtypeStruct((t, D_MODEL), dt) for dt in out_dtypes],
        compiler_params=pltpu.CompilerParams(
            dimension_semantics=("arbitrary",), vmem_limit_bytes=VMEM_LIMIT),
        name="inproj",
    )(x2d, ada, norm_w, w_in, pos, invf, lb_logits, rec_norm_row)


def _attn_kernel(lq1_ref, lk1_ref, lq2_ref, lk2_ref, sw_ref, q_ref, k_ref, v_ref, o_ref,
                 vt_scr, acc, m, l, dev, *, tq):
    qi = pl.program_id(2)
    n_tiles = pl.num_programs(2)
    n_streams = 2 * HEADS_PER_STEP

    @pl.when(qi == 0)
    def _():
        def tr(j, carry):
            start = pl.multiple_of(j * tq, tq)
            vt_scr[:, pl.ds(start, tq)] = v_ref[0, pl.ds(start, tq), :].T
            return carry
        lax.fori_loop(0, n_tiles, tr, 0)

    def head_lanes(s):
        return slice((s // 2) * HEAD_W, (s // 2 + 1) * HEAD_W)

    q = q_ref[0]
    lane = lax.broadcasted_iota(jnp.int32, (1, HEAD_W), 1)
    q_streams = []
    for s in range(n_streams):
        qh = q[:, head_lanes(s)]
        keep = (lane < QK_DIM) if s % 2 == 0 else (lane >= QK_DIM)
        q_streams.append(jnp.where(keep, qh, jnp.zeros_like(qh)))

    def init():
        acc[...] = jnp.zeros_like(acc)
        m[...] = jnp.full_like(m, -jnp.inf)
        l[...] = jnp.zeros_like(l)
        dev[...] = jnp.zeros_like(dev)

    def tiles(kj):
        start = pl.multiple_of(kj * tq, tq)
        return k_ref[0, pl.ds(start, tq), :], vt_scr[:, pl.ds(start, tq)]

    def exact_step(kj):
        k, vt = tiles(kj)
        for s in range(n_streams):
            hl = head_lanes(s)
            st = lax.dot_general(k[:, hl], q_streams[s], _NT, preferred_element_type=F32)
            kr = lax.broadcasted_iota(jnp.int32, st.shape, 0) + (kj - qi) * tq
            qc = lax.broadcasted_iota(jnp.int32, st.shape, 1)
            st = jnp.where(kr <= qc, st, -jnp.inf)
            m_prev = m[s]
            m_new = jnp.maximum(m_prev, jnp.max(st, axis=0, keepdims=True))
            alpha = jnp.exp2(m_prev - m_new)
            pt = jnp.exp2(st - m_new)
            l[s] = alpha * l[s] + jnp.sum(pt, axis=0, keepdims=True)
            acc[s] = alpha * acc[s] + jnp.dot(vt[hl, :], pt.astype(BF16), preferred_element_type=F32)
            m[s] = m_new

    def stream_tiles(kj0, n, after_first_scores=None):
        kv = [tiles(kj0 + t) for t in range(n)]
        n_sub = tq // SUB_KEYS
        chains = [(t, s, h) for t in range(n) for s in range(n_streams) for h in range(n_sub)]

        def qk(c):
            t, s, h = c
            ks = slice(h * SUB_KEYS, (h + 1) * SUB_KEYS)
            return lax.dot_general(kv[t][0][ks, head_lanes(s)], q_streams[s], _NT,
                                   preferred_element_type=F32)

        scores = {i: qk(c) for i, c in enumerate(chains[:QK_AHEAD])}
        if after_first_scores is not None:
            after_first_scores()
        part = {}
        for i, (t, s, h) in enumerate(chains):
            st = scores.pop(i)
            m_prev = m[s]
            pt = jnp.exp2(st - m_prev)
            mx = jnp.max(st, axis=0, keepdims=True)
            sm = jnp.sum(pt, axis=0, keepdims=True)
            if i + QK_AHEAD < len(chains):
                scores[i + QK_AHEAD] = qk(chains[i + QK_AHEAD])
            ks = slice(h * SUB_KEYS, (h + 1) * SUB_KEYS)
            pp = jnp.dot(kv[t][1][head_lanes(s), ks], pt.astype(BF16), preferred_element_type=F32)
            if h == 0:
                part[s] = (mx, sm, pp)
            else:
                mx0, sm0, pp0 = part[s]
                part[s] = (jnp.maximum(mx0, mx), sm0 + sm, pp0 + pp)
            if h == n_sub - 1:
                t_max, l_add, pv = part.pop(s)
                m_new = jnp.maximum(m_prev, t_max)
                alpha = jnp.exp2(m_prev - m_new)
                dev[s] = jnp.maximum(dev[s], t_max - m_prev)
                l[s] = alpha * (l[s] + l_add)
                acc[s] = alpha * (acc[s] + pv)
                m[s] = m_new

    def diag_step(then_stream):
        k, vt = tiles(qi)
        half = tq // 2
        problems = ((slice(0, half), slice(0, half)), (slice(0, tq), slice(half, tq)))
        sts = {(s, pi): lax.dot_general(k[ks, head_lanes(s)], q_streams[s][qs], _NT,
                                        preferred_element_type=F32)
               for s in range(n_streams) for pi, (ks, qs) in enumerate(problems)}
        dev[...] = jnp.zeros_like(dev)
        pending = []
        for s in range(n_streams):
            for pi, (ks, qs) in enumerate(problems):
                st = sts[s, pi]
                kr = lax.broadcasted_iota(jnp.int32, st.shape, 0)
                qc = lax.broadcasted_iota(jnp.int32, st.shape, 1) + qs.start
                st = jnp.where(kr <= qc, st, -jnp.inf)
                m_new = jnp.max(st, axis=0, keepdims=True)
                pt = jnp.exp2(st - m_new)
                l[s, :, qs] = jnp.sum(pt, axis=0, keepdims=True)
                m[s, :, qs] = m_new
                pending.append((s, ks, qs, pt.astype(BF16)))

        def value_matmuls():
            for s, ks, qs, ptb in pending:
                acc[s, :, qs] = jnp.dot(vt[head_lanes(s), ks], ptb, preferred_element_type=F32)

        if then_stream:
            stream_tiles(0, then_stream, after_first_scores=value_matmuls)
        else:
            value_matmuls()

    first = qi % STREAM_TILES
    for r in range(STREAM_TILES):
        @pl.when(first == r)
        def _(r=r):
            diag_step(r)

    def body(t, carry):
        stream_tiles(first + STREAM_TILES * t, STREAM_TILES)
        return carry

    lax.fori_loop(0, qi // STREAM_TILES, body, 0)

    lam = (jnp.exp(jnp.sum(lq1_ref[...] * lk1_ref[...], axis=-1, keepdims=True))
           - jnp.exp(jnp.sum(lq2_ref[...] * lk2_ref[...], axis=-1, keepdims=True)) + LAM_INIT)

    def write_output():
        for hh in range(HEADS_PER_STEP):
            s1, s2 = 2 * hh, 2 * hh + 1
            ot = acc[s1] / l[s1] - lam * (acc[s2] / l[s2])
            inv = lax.rsqrt(jnp.mean(ot * ot, axis=0, keepdims=True) + EPS)
            ot = ot * inv * sw_ref[...] * (1.0 - LAM_INIT)
            o_ref[0, hh * HEAD_W:(hh + 1) * HEAD_W, :] = ot.astype(o_ref.dtype)

    write_output()

    rise = jnp.max(dev[...])

    @pl.when(jnp.logical_not(rise <= MAX_RISE))
    def _():
        init()

        def redo(kj, carry):
            exact_step(kj)
            return carry
        lax.fori_loop(0, qi + 1, redo, 0)
        write_output()


def _attn(q3, k3, v3, lq1, lk1, lq2, lk2, subln_col, *, tq):
    b, s, _ = q3.shape
    vec = lambda bi, h, qi: (0, 0)
    width = HEADS_PER_STEP * HEAD_W
    n_streams = 2 * HEADS_PER_STEP
    kernel = functools.partial(_attn_kernel, tq=tq)
    return pl.pallas_call(
        kernel,
        grid=(b, HEADS // HEADS_PER_STEP, s // tq),
        in_specs=[pl.BlockSpec((1, QK_DIM), vec)] * 4
                 + [pl.BlockSpec((HEAD_W, 1), vec),
                    pl.BlockSpec((1, tq, width), lambda bi, h, qi: (bi, qi, h)),
                    pl.BlockSpec((1, s, width), lambda bi, h, qi: (bi, 0, h)),
                    pl.BlockSpec((1, s, width), lambda bi, h, qi: (bi, 0, h))],
        out_specs=pl.BlockSpec((1, width, tq), lambda bi, h, qi: (bi, h, qi)),
        out_shape=jax.ShapeDtypeStruct((b, D_MODEL, s), BF16),
        scratch_shapes=[pltpu.VMEM((width, s), BF16),
                        pltpu.VMEM((n_streams, HEAD_W, tq), F32),
                        pltpu.VMEM((n_streams, 1, tq), F32),
                        pltpu.VMEM((n_streams, 1, tq), F32),
                        pltpu.VMEM((n_streams, 1, tq), F32)],
        compiler_params=pltpu.CompilerParams(
            dimension_semantics=("arbitrary", "arbitrary", "arbitrary"), vmem_limit_bytes=VMEM_LIMIT),
        name="attn",
    )(lq1, lk1, lq2, lk2, subln_col, q3, k3, v3)


def _hgrn_decay(g_all, n_chunks):
    n_rows = n_chunks * REC_CHUNK
    t_row = lax.broadcasted_iota(jnp.int32, (n_rows, n_rows), 0)
    t_col = lax.broadcasted_iota(jnp.int32, (n_rows, n_rows), 1)
    tri = ((t_row // REC_CHUNK == t_col // REC_CHUNK) & (t_row >= t_col)).astype(BF16)
    g_hi = g_all.astype(BF16)
    rest = g_all - g_hi.astype(F32)
    g_mid = rest.astype(BF16)
    g_lo = (rest - g_mid.astype(F32)).astype(BF16)
    return (jnp.dot(tri, g_hi, preferred_element_type=F32)
            + jnp.dot(tri, g_mid, preferred_element_type=F32)
            + jnp.dot(tri, g_lo, preferred_element_type=F32))


def _hgrn_tile(rq_ref, g_all, b_all, ri_ref, gate_ref, o_ref, st_scr, sc_scr, n_chunks):
    c_len = REC_CHUNK

    rows_i = lax.broadcasted_iota(jnp.int32, (c_len, c_len), 0)
    cols_i = lax.broadcasted_iota(jnp.int32, (c_len, c_len), 1)
    causal = rows_i >= cols_i
    row_col = lax.broadcasted_iota(jnp.int32, (c_len, 1), 0)

    def prepare(c):
        rows = slice(c * c_len, (c + 1) * c_len)
        g = g_all[rows]
        b = b_all[rows]
        b_last = b[c_len - 1:c_len, :]
        b_mid = b[c_len // 2 - 1:c_len // 2, :]
        safe = jnp.maximum(jnp.max(-b_mid), jnp.max(b_mid - b_last)) < SAFE_DECAY
        return rows, g, b, b_last, b_mid, safe

    def factored_scores(q, kk, b, b_mid):
        qb = (q * jnp.exp2(b - b_mid)).astype(BF16)
        kh = (kk * jnp.exp2(b_mid - b)).astype(BF16)

        def head(h):
            sl = slice(h * HEAD_W, (h + 1) * HEAD_W)
            s = lax.dot_general(qb[:, sl], kh[:, sl], _NT, preferred_element_type=F32)
            return jnp.where(causal, s, 0.0)
        return head

    def exact_scores(q, kk, b):
        def head(h):
            sl = slice(h * HEAD_W, (h + 1) * HEAD_W)
            qh, bh, kh = q[:, sl], b[:, sl], kk[:, sl]

            def col_body(s_idx, scores):
                pick = row_col == s_idx
                b_row = jnp.sum(jnp.where(pick, bh, 0.0), axis=0, keepdims=True)
                k_row = jnp.sum(jnp.where(pick, kh, 0.0), axis=0, keepdims=True)
                e = jnp.exp2(jnp.minimum(bh - b_row, 0.0))
                col = jnp.sum(qh * k_row * e, axis=1, keepdims=True)
                col = jnp.where(row_col >= s_idx, col, 0.0)
                return jnp.where(cols_i == s_idx, col, scores)

            return lax.fori_loop(0, c_len, col_body, jnp.zeros((c_len, c_len), F32))
        return head

    def finish(rows, q, kk, b, b_last, scores_of_head, state):
        qb = (q * jnp.exp2(b)).astype(BF16)
        kdb = (kk * jnp.exp2(b_last - b)).astype(BF16)
        decay_last = jnp.exp2(b_last)
        lanes = [slice(h * HEAD_W, (h + 1) * HEAD_W) for h in range(HEADS)]
        vs = [ri_ref[rows, sl] for sl in lanes]
        sc = [scores_of_head(h).astype(BF16) for h in range(HEADS)]
        upd = [lax.dot_general(vs[h], kdb[:, lanes[h]], _TN, preferred_element_type=F32)
               for h in range(HEADS)]
        intra = [jnp.dot(sc[h], vs[h], preferred_element_type=F32) for h in range(HEADS)]
        inter = [lax.dot_general(qb[:, lanes[h]], state[h].astype(BF16), _NT,
                                 preferred_element_type=F32) for h in range(HEADS)]
        for h, sl in enumerate(lanes):
            state[h] = state[h] * decay_last[:, sl] + upd[h]
            o = intra[h] + inter[h]
            o_ref[rows, sl] = (_rms(o) * gate_ref[rows, sl].astype(F32)).astype(o_ref.dtype)

    prepared = [prepare(c) for c in range(n_chunks)]
    all_safe = functools.reduce(jnp.logical_and, [p[5] for p in prepared])

    state = [st_scr[h] for h in range(HEADS)]
    for rows, g, b, b_last, b_mid, _ in prepared:
        q = rq_ref[rows, :].astype(F32)
        kk = 1.0 - jnp.exp2(g)
        finish(rows, q, kk, b, b_last, factored_scores(q, kk, b, b_mid), state)

    @pl.when(all_safe)
    def _():
        for h in range(HEADS):
            st_scr[h] = state[h]

    @pl.when(jnp.logical_not(all_safe))
    def _():
        redo_state = [st_scr[h] for h in range(HEADS)]
        for rows, g, b, b_last, b_mid, safe in prepared:
            q = rq_ref[rows, :].astype(F32)
            kk = 1.0 - jnp.exp2(g)

            @pl.when(safe)
            def _():
                head = factored_scores(q, kk, b, b_mid)
                for h in range(HEADS):
                    sc_scr[h] = head(h)

            @pl.when(jnp.logical_not(safe))
            def _():
                head = exact_scores(q, kk, b)
                for h in range(HEADS):
                    sc_scr[h] = head(h)

            finish(rows, q, kk, b, b_last, lambda h: sc_scr[h], redo_state)
        for h in range(HEADS):
            st_scr[h] = redo_state[h]


def _mix_kernel(rq_ref, g_ref, ri_ref, gate_ref, oa_ref, ga_ref, gr_ref, x_ref, ada_ref,
                wa_ref, wr_ref, wo_ref, x1_ref, st_scr, sc_scr, or_scr, *, n_chunks, tiles_per_batch):
    i = pl.program_id(0)

    @pl.when(i % tiles_per_batch == 0)
    def _():
        st_scr[...] = jnp.zeros_like(st_scr)

    @pl.when(i == 0)
    def _():
        or_scr[...] = jnp.zeros_like(or_scr)

    g_all = g_ref[...]
    b_all = _hgrn_decay(g_all, n_chunks)

    o_prev = or_scr[...]
    ya = lax.dot_general(oa_ref[0], wa_ref[...], _TN, preferred_element_type=F32)
    yr = jnp.dot(o_prev, wr_ref[...], preferred_element_type=F32)
    y = ga_ref[...].astype(F32) * ya + gr_ref[...].astype(F32) * yr
    upd = jnp.dot(y.astype(BF16), wo_ref[...], preferred_element_type=F32)
    x1_ref[...] = x_ref[...] + ada_ref[0][2:3] * upd

    _hgrn_tile(rq_ref, g_all, b_all, ri_ref, gate_ref, or_scr, st_scr, sc_scr, n_chunks)


def _mix(rq, g, ri, gate, oa_t, ga, gr, x2d, ada, wa, wr, wo, *, seq, tt):
    t = x2d.shape[0]
    n_tiles = t // tt
    tiles_per_batch = seq // tt
    cur = lambda i: (jnp.minimum(i, n_tiles - 1), 0)
    prev = lambda i: (jnp.maximum(i - 1, 0), 0)
    wspec = pl.BlockSpec((D_MODEL, D_MODEL), lambda i: (0, 0))
    kernel = functools.partial(_mix_kernel, n_chunks=tt // REC_CHUNK, tiles_per_batch=tiles_per_batch)
    return pl.pallas_call(
        kernel,
        grid=(n_tiles + 1,),
        in_specs=[pl.BlockSpec((tt, D_MODEL), cur)] * 4
                 + [pl.BlockSpec((1, D_MODEL, tt),
                                 lambda i: (jnp.maximum(i - 1, 0) // tiles_per_batch, 0,
                                            jnp.maximum(i - 1, 0) % tiles_per_batch))]
                 + [pl.BlockSpec((tt, D_MODEL), prev)] * 3
                 + [pl.BlockSpec((1, 8, D_MODEL),
                                 lambda i: (jnp.maximum(i - 1, 0) // tiles_per_batch, 0, 0)),
                    wspec, wspec, wspec],
        out_specs=pl.BlockSpec((tt, D_MODEL), prev),
        out_shape=jax.ShapeDtypeStruct((t, D_MODEL), F32),
        scratch_shapes=[pltpu.VMEM((HEADS, HEAD_W, HEAD_W), F32),
                        pltpu.VMEM((HEADS, REC_CHUNK, REC_CHUNK), F32),
                        pltpu.VMEM((tt, D_MODEL), BF16)],
        compiler_params=pltpu.CompilerParams(
            dimension_semantics=("arbitrary",), vmem_limit_bytes=VMEM_LIMIT),
        name="mix",
    )(rq, g, ri, gate, oa_t, ga, gr, x2d, ada, wa, wr, wo)


def _mlp_kernel(x_ref, ada_ref, nw_ref, nf_ref, w1_ref, w2_ref, o_ref):
    x = x_ref[...]
    ada = ada_ref[0]
    h = (_rms(x) * nw_ref[...] * (1.0 + ada[4:5]) + ada[3:4]).astype(BF16)
    acc = jnp.zeros_like(x)
    for c in range(D_FF // D_MODEL):
        cols = slice(c * D_MODEL, (c + 1) * D_MODEL)
        u = jnp.dot(h, w1_ref[:, cols], preferred_element_type=F32)
        u = jnp.square(jnp.maximum(u, 0.0))
        acc = acc + jnp.dot(u.astype(BF16), w2_ref[cols, :], preferred_element_type=F32)
    o_ref[...] = _rms(x + ada[5:6] * acc) * nf_ref[...]


def _mlp(x1, ada, norm_w, norm_final, w1, w2, *, seq, tm):
    t = x1.shape[0]
    tiles_per_batch = seq // tm
    row = lambda i: (i, 0)
    const = lambda i: (0, 0)
    return pl.pallas_call(
        _mlp_kernel,
        grid=(t // tm,),
        in_specs=[pl.BlockSpec((tm, D_MODEL), row),
                  pl.BlockSpec((1, 8, D_MODEL), lambda i: (i // tiles_per_batch, 0, 0)),
                  pl.BlockSpec((1, D_MODEL), const),
                  pl.BlockSpec((1, D_MODEL), const),
                  pl.BlockSpec((D_MODEL, D_FF), const, pipeline_mode=pl.Buffered(1)),
                  pl.BlockSpec((D_FF, D_MODEL), const, pipeline_mode=pl.Buffered(1))],
        out_specs=pl.BlockSpec((tm, D_MODEL), row),
        out_shape=jax.ShapeDtypeStruct((t, D_MODEL), F32),
        compiler_params=pltpu.CompilerParams(
            dimension_semantics=("arbitrary",), vmem_limit_bytes=VMEM_LIMIT),
        name="mlp",
    )(x1, ada, norm_w, norm_final, w1, w2)


def _rope_inv_freq_lanes():
    inv_freq = ROPE_THETA ** (-jnp.arange(0, ROPE_DIM, 2, dtype=F32) / ROPE_DIM)
    lane = jnp.arange(HEAD_W) % QK_DIM
    table = jnp.where(lane < ROPE_DIM, inv_freq[lane % (ROPE_DIM // 2)], 0.0)
    return table.reshape(1, HEAD_W).astype(F32)


def kernel(x, c, positions, w_ada, b_ada, norm_mix, w_in, lam_q1, lam_k1, lam_q2, lam_k2, subln_w, lb_logits, rec_norm_w, w_proj_att, w_proj_rec, w_out, norm_mlp, w_mlp_in, w_mlp_out, norm_final):
    batch, seq, d = x.shape
    assert d == D_MODEL and w_ada.shape[0] == 1, "single-layer, d_model=1024 only"
    assert batch <= 8
    t = batch * seq
    tm = min(512, seq)
    tq = min(512, seq)
    tt = min(256, seq)
    assert seq % tm == 0 and seq % tq == 0 and seq % tt == 0 and tt % REC_CHUNK == 0

    x2d = x.reshape(t, D_MODEL)
    c_pad = jnp.zeros((8, D_MODEL), F32).at[:batch].set(c)
    ada = _ada(c_pad, w_ada[0], b_ada[0].reshape(1, -1))[:batch]
    ada = jnp.pad(ada.reshape(batch, N_ADA, D_MODEL), ((0, 0), (0, 8 - N_ADA), (0, 0)))

    q, k, v, rq, g, ri, gate, ga, gr = _inproj(
        x2d, ada, norm_mix[0].reshape(1, -1), w_in[0].astype(BF16),
        positions.reshape(t, 1), _rope_inv_freq_lanes(), lb_logits,
        jnp.tile(rec_norm_w.reshape(1, HEAD_W), (1, HEADS)), seq=seq, tm=tm)

    as3 = lambda a: a.reshape(batch, seq, D_MODEL)
    o_a = _attn(as3(q), as3(k), as3(v), lam_q1, lam_k1, lam_q2, lam_k2,
                subln_w.reshape(HEAD_W, 1), tq=tq)
    x1 = _mix(rq, g, ri, gate, o_a, ga, gr, x2d, ada,
              w_proj_att[0].astype(BF16), w_proj_rec[0].astype(BF16), w_out[0].astype(BF16),
              seq=seq, tt=tt)
    out = _mlp(x1, ada, norm_mlp[0].reshape(1, -1), norm_final.reshape(1, -1),
               w_mlp_in[0].astype(BF16), w_mlp_out[0].astype(BF16), seq=seq, tm=tm)
    return out.reshape(batch, seq, D_MODEL)
```

```python
import functools
import math

import jax
import jax.numpy as jnp
from jax import lax
from jax.experimental import pallas as pl
from jax.experimental.pallas import tpu as pltpu

F32 = jnp.float32
BF16 = jnp.bfloat16

D_MODEL = 1024
HEADS = 8
HEAD_W = 128
QK_DIM = 64
ROPE_DIM = 16
ROPE_THETA = 500000.0
D_FF = 4 * D_MODEL
N_ADA = 6
N_SECTIONS = 9
REC_CHUNK = 64
EPS = 1e-6
LAM_INIT = 0.8 - 0.6 * math.exp(-0.3 * 0)
LOG2_E = 1.0 / math.log(2.0)
SAFE_LOG2 = 115.0
MAX_RISE = 64.0
SUB_KEYS = 256
STREAM_TILES = 4
QK_AHEAD = 2
HEADS_PER_STEP = 4
ROW_GROUPS = 2
VMEM_LIMIT = 56 * 1024 * 1024

_NT = (((1,), (1,)), ((), ()))
_TN = (((0,), (0,)), ((), ()))


def _rms(x):
    return x * lax.rsqrt(jnp.mean(x * x, axis=-1, keepdims=True) + EPS)


def _ada_kernel(c_ref, w_ref, b_ref, o_ref):
    c = c_ref[...]
    cond = c * jax.nn.sigmoid(c)
    o_ref[...] = jnp.dot(cond, w_ref[...], preferred_element_type=F32,
                         precision=lax.Precision.HIGHEST) + b_ref[...]


def _ada(c_pad, w_ada, b_ada):
    n = w_ada.shape[1]
    return pl.pallas_call(
        _ada_kernel,
        grid=(n // D_MODEL,),
        in_specs=[pl.BlockSpec((8, D_MODEL), lambda j: (0, 0)),
                  pl.BlockSpec((D_MODEL, D_MODEL), lambda j: (0, j)),
                  pl.BlockSpec((1, D_MODEL), lambda j: (0, j))],
        out_specs=pl.BlockSpec((8, D_MODEL), lambda j: (0, j)),
        out_shape=jax.ShapeDtypeStruct((8, n), F32),
        name="ada",
    )(c_pad, w_ada, b_ada)


def _inproj_kernel(x_ref, ada_ref, nw_ref, w_ref, pos_ref, invf_ref, lb_ref, rnw_ref,
                   q_ref, k_ref, v_ref, rq_ref, g_ref, ri_ref, gate_ref, ga_ref, gr_ref):
    ada = ada_ref[0]
    rows = x_ref.shape[0] // ROW_GROUPS
    hs = [(_rms(x_ref[r * rows:(r + 1) * rows, :]) * nw_ref[...] * (1.0 + ada[1:2]) + ada[0:1]).astype(BF16)
          for r in range(ROW_GROUPS)]

    ang = pos_ref[...].astype(F32) * invf_ref[...]
    lane = lax.broadcasted_iota(jnp.int32, (1, HEAD_W), 1) % QK_DIM
    cos_a = jnp.cos(ang)
    sin_a = jnp.sin(ang)
    cs = jnp.where(lane < ROPE_DIM, cos_a, 1.0)
    s1 = jnp.where(lane < ROPE_DIM // 2, -sin_a, 0.0)
    s2 = jnp.where((lane >= ROPE_DIM // 2) & (lane < ROPE_DIM), sin_a, 0.0)

    def section(j):
        w = w_ref[:, j * D_MODEL:(j + 1) * D_MODEL]
        return jnp.concatenate([jnp.dot(hh, w, preferred_element_type=F32) for hh in hs], axis=0)

    def rope_store(ref, acc, scale):
        c_, s1_, s2_ = cs * scale, s1 * scale, s2 * scale
        for hh in range(HEADS):
            t = acc[:, hh * HEAD_W:(hh + 1) * HEAD_W]
            r = (t * c_ + pltpu.roll(t, HEAD_W - ROPE_DIM // 2, 1) * s1_
                 + pltpu.roll(t, ROPE_DIM // 2, 1) * s2_)
            ref[:, hh * HEAD_W:(hh + 1) * HEAD_W] = r.astype(ref.dtype)

    v_ref[...] = section(2).astype(v_ref.dtype)
    rope_store(q_ref, section(0), QK_DIM ** -0.5 * LOG2_E)
    rq_ref[...] = section(3).astype(rq_ref.dtype)
    rope_store(k_ref, section(1), 1.0)
    ri_ref[...] = section(5).astype(ri_ref.dtype)

    l0 = lb_ref[0:1, :]
    l1 = lb_ref[1:2, :]
    mx = jnp.maximum(l0, l1)
    e0 = jnp.exp(l0 - mx)
    e1 = jnp.exp(l1 - mx)
    lb = e1 / (e0 + e1)
    g_ref[...] = jnp.log(lb + (1.0 - lb) * jax.nn.sigmoid(section(4))) * LOG2_E

    rg = section(6)
    gate_ref[...] = (rnw_ref[...] * (rg * jax.nn.sigmoid(rg))).astype(gate_ref.dtype)
    ga_ref[...] = jax.nn.sigmoid(section(7)).astype(ga_ref.dtype)
    gr_ref[...] = jax.nn.sigmoid(section(8)).astype(gr_ref.dtype)


def _inproj(x2d, ada, norm_w, w_in, pos, invf, lb_logits, rec_norm_row, *, seq, tm):
    t = x2d.shape[0]
    tiles_per_batch = seq // tm
    row = lambda i: (i, 0)
    const2 = lambda i: (0, 0)
    out_dtypes = (BF16, BF16, BF16, BF16, F32, BF16, BF16, BF16, BF16)
    return pl.pallas_call(
        _inproj_kernel,
        grid=(t // tm,),
        in_specs=[pl.BlockSpec((tm, D_MODEL), row),
                  pl.BlockSpec((1, 8, D_MODEL), lambda i: (i // tiles_per_batch, 0, 0)),
                  pl.BlockSpec((1, D_MODEL), const2),
                  pl.BlockSpec((D_MODEL, N_SECTIONS * D_MODEL), const2, pipeline_mode=pl.Buffered(1)),
                  pl.BlockSpec((tm, 1), row),
                  pl.BlockSpec((1, HEAD_W), const2),
                  pl.BlockSpec((2, D_MODEL), const2),
                  pl.BlockSpec((1, D_MODEL), const2)],
        out_specs=[pl.BlockSpec((tm, D_MODEL), row) for _ in out_dtypes],
        out_shape=[jax.ShapeDtypeStruct((t, D_MODEL), dt) for dt in out_dtypes],
        compiler_params=pltpu.CompilerParams(
            dimension_semantics=("arbitrary",), vmem_limit_bytes=VMEM_LIMIT),
        name="inproj",
    )(x2d, ada, norm_w, w_in, pos, invf, lb_logits, rec_norm_row)


def _attn_kernel(lq1_ref, lk1_ref, lq2_ref, lk2_ref, sw_ref, q_ref, k_ref, v_ref, o_ref,
                 vt_scr, acc, m, l, dev, *, tq):
    qi = pl.program_id(2)
    n_tiles = pl.num_programs(2)
    n_streams = 2 * HEADS_PER_STEP

    @pl.when(qi == 0)
    def _():
        def tr(j, carry):
            start = pl.multiple_of(j * tq, tq)
            vt_scr[:, pl.ds(start, tq)] = v_ref[0, pl.ds(start, tq), :].T
            return carry
        lax.fori_loop(0, n_tiles, tr, 0)

    def head_lanes(s):
        return slice((s // 2) * HEAD_W, (s // 2 + 1) * HEAD_W)

    q = q_ref[0]
    lane = lax.broadcasted_iota(jnp.int32, (1, HEAD_W), 1)
    q_streams = []
    for s in range(n_streams):
        qh = q[:, head_lanes(s)]
        keep = (lane < QK_DIM) if s % 2 == 0 else (lane >= QK_DIM)
        q_streams.append(jnp.where(keep, qh, jnp.zeros_like(qh)))

    def init():
        acc[...] = jnp.zeros_like(acc)
        m[...] = jnp.full_like(m, -jnp.inf)
        l[...] = jnp.zeros_like(l)
        dev[...] = jnp.zeros_like(dev)

    def tiles(kj):
        start = pl.multiple_of(kj * tq, tq)
        return k_ref[0, pl.ds(start, tq), :], vt_scr[:, pl.ds(start, tq)]

    def exact_step(kj):
        k, vt = tiles(kj)
        for s in range(n_streams):
            hl = head_lanes(s)
            st = lax.dot_general(k[:, hl], q_streams[s], _NT, preferred_element_type=F32)
            kr = lax.broadcasted_iota(jnp.int32, st.shape, 0) + (kj - qi) * tq
            qc = lax.broadcasted_iota(jnp.int32, st.shape, 1)
            st = jnp.where(kr <= qc, st, -jnp.inf)
            m_prev = m[s]
            m_new = jnp.maximum(m_prev, jnp.max(st, axis=0, keepdims=True))
            alpha = jnp.exp2(m_prev - m_new)
            pt = jnp.exp2(st - m_new)
            l[s] = alpha * l[s] + jnp.sum(pt, axis=0, keepdims=True)
            acc[s] = alpha * acc[s] + jnp.dot(vt[hl, :], pt.astype(BF16), preferred_element_type=F32)
            m[s] = m_new

    def stream_tiles(kj0, n, after_first_scores=None):
        kv = [tiles(kj0 + t) for t in range(n)]
        n_sub = tq // SUB_KEYS
        chains = [(t, s, h) for t in range(n) for s in range(n_streams) for h in range(n_sub)]

        def qk(c):
            t, s, h = c
            ks = slice(h * SUB_KEYS, (h + 1) * SUB_KEYS)
            return lax.dot_general(kv[t][0][ks, head_lanes(s)], q_streams[s], _NT,
                                   preferred_element_type=F32)

        scores = {i: qk(c) for i, c in enumerate(chains[:QK_AHEAD])}
        if after_first_scores is not None:
            after_first_scores()
        part = {}
        for i, (t, s, h) in enumerate(chains):
            st = scores.pop(i)
            m_prev = m[s]
            pt = jnp.exp2(st - m_prev)
            mx = jnp.max(st, axis=0, keepdims=True)
            sm = jnp.sum(pt, axis=0, keepdims=True)
            if i + QK_AHEAD < len(chains):
                scores[i + QK_AHEAD] = qk(chains[i + QK_AHEAD])
            ks = slice(h * SUB_KEYS, (h + 1) * SUB_KEYS)
            pp = jnp.dot(kv[t][1][head_lanes(s), ks], pt.astype(BF16), preferred_element_type=F32)
            if h == 0:
                part[s] = (mx, sm, pp)
            else:
                mx0, sm0, pp0 = part[s]
                part[s] = (jnp.maximum(mx0, mx), sm0 + sm, pp0 + pp)
            if h == n_sub - 1:
                t_max, l_add, pv = part.pop(s)
                m_new = jnp.maximum(m_prev, t_max)
                alpha = jnp.exp2(m_prev - m_new)
                dev[s] = jnp.maximum(dev[s], t_max - m_prev)
                l[s] = alpha * (l[s] + l_add)
                acc[s] = alpha * (acc[s] + pv)
                m[s] = m_new

    def diag_step(then_stream):
        k, vt = tiles(qi)
        half = tq // 2
        problems = ((slice(0, half), slice(0, half)), (slice(0, tq), slice(half, tq)))
        sts = {(s, pi): lax.dot_general(k[ks, head_lanes(s)], q_streams[s][qs], _NT,
                                        preferred_element_type=F32)
               for s in range(n_streams) for pi, (ks, qs) in enumerate(problems)}
        dev[...] = jnp.zeros_like(dev)
        pending = []
        for s in range(n_streams):
            for pi, (ks, qs) in enumerate(problems):
                st = sts[s, pi]
                kr = lax.broadcasted_iota(jnp.int32, st.shape, 0)
                qc = lax.broadcasted_iota(jnp.int32, st.shape, 1) + qs.start
                st = jnp.where(kr <= qc, st, -jnp.inf)
                m_new = jnp.max(st, axis=0, keepdims=True)
                pt = jnp.exp2(st - m_new)
                l[s, :, qs] = jnp.sum(pt, axis=0, keepdims=True)
                m[s, :, qs] = m_new
                pending.append((s, ks, qs, pt.astype(BF16)))

        def value_matmuls():
            for s, ks, qs, ptb in pending:
                acc[s, :, qs] = jnp.dot(vt[head_lanes(s), ks], ptb, preferred_element_type=F32)

        if then_stream:
            stream_tiles(0, then_stream, after_first_scores=value_matmuls)
        else:
            value_matmuls()

    first = qi % STREAM_TILES
    for r in range(STREAM_TILES):
        @pl.when(first == r)
        def _(r=r):
            diag_step(r)

    def body(t, carry):
        stream_tiles(first + STREAM_TILES * t, STREAM_TILES)
        return carry

    lax.fori_loop(0, qi // STREAM_TILES, body, 0)

    lam = (jnp.exp(jnp.sum(lq1_ref[...] * lk1_ref[...], axis=-1, keepdims=True))
           - jnp.exp(jnp.sum(lq2_ref[...] * lk2_ref[...], axis=-1, keepdims=True)) + LAM_INIT)

    def write_output():
        for hh in range(HEADS_PER_STEP):
            s1, s2 = 2 * hh, 2 * hh + 1
            ot = acc[s1] / l[s1] - lam * (acc[s2] / l[s2])
            inv = lax.rsqrt(jnp.mean(ot * ot, axis=0, keepdims=True) + EPS)
            ot = ot * inv * sw_ref[...] * (1.0 - LAM_INIT)
            o_ref[0, hh * HEAD_W:(hh + 1) * HEAD_W, :] = ot.astype(o_ref.dtype)

    write_output()

    rise = jnp.max(dev[...])

    @pl.when(jnp.logical_not(rise <= MAX_RISE))
    def _():
        init()

        def redo(kj, carry):
            exact_step(kj)
            return carry
        lax.fori_loop(0, qi + 1, redo, 0)
        write_output()


def _attn(q3, k3, v3, lq1, lk1, lq2, lk2, subln_col, *, tq):
    b, s, _ = q3.shape
    vec = lambda bi, h, qi: (0, 0)
    width = HEADS_PER_STEP * HEAD_W
    n_streams = 2 * HEADS_PER_STEP
    kernel = functools.partial(_attn_kernel, tq=tq)
    return pl.pallas_call(
        kernel,
        grid=(b, HEADS // HEADS_PER_STEP, s // tq),
        in_specs=[pl.BlockSpec((1, QK_DIM), vec)] * 4
                 + [pl.BlockSpec((HEAD_W, 1), vec),
                    pl.BlockSpec((1, tq, width), lambda bi, h, qi: (bi, qi, h)),
                    pl.BlockSpec((1, s, width), lambda bi, h, qi: (bi, 0, h)),
                    pl.BlockSpec((1, s, width), lambda bi, h, qi: (bi, 0, h))],
        out_specs=pl.BlockSpec((1, width, tq), lambda bi, h, qi: (bi, h, qi)),
        out_shape=jax.ShapeDtypeStruct((b, D_MODEL, s), BF16),
        scratch_shapes=[pltpu.VMEM((width, s), BF16),
                        pltpu.VMEM((n_streams, HEAD_W, tq), F32),
                        pltpu.VMEM((n_streams, 1, tq), F32),
                        pltpu.VMEM((n_streams, 1, tq), F32),
                        pltpu.VMEM((n_streams, 1, tq), F32)],
        compiler_params=pltpu.CompilerParams(
            dimension_semantics=("arbitrary", "arbitrary", "arbitrary"), vmem_limit_bytes=VMEM_LIMIT),
        name="attn",
    )(lq1, lk1, lq2, lk2, subln_col, q3, k3, v3)


def _hgrn_kernel(rq_ref, g_ref, ri_ref, gate_ref, o_ref, st_scr, sc_scr, *, n_chunks):
    c_len = REC_CHUNK

    @pl.when(pl.program_id(1) == 0)
    def _():
        st_scr[...] = jnp.zeros_like(st_scr)

    rows_i = lax.broadcasted_iota(jnp.int32, (c_len, c_len), 0)
    cols_i = lax.broadcasted_iota(jnp.int32, (c_len, c_len), 1)
    causal = rows_i >= cols_i
    row_col = lax.broadcasted_iota(jnp.int32, (c_len, 1), 0)

    n_rows = n_chunks * c_len
    t_row = lax.broadcasted_iota(jnp.int32, (n_rows, n_rows), 0)
    t_col = lax.broadcasted_iota(jnp.int32, (n_rows, n_rows), 1)
    tri = ((t_row // c_len == t_col // c_len) & (t_row >= t_col)).astype(BF16)
    g_all = g_ref[...]
    g_hi = g_all.astype(BF16)
    rest = g_all - g_hi.astype(F32)
    g_mid = rest.astype(BF16)
    g_lo = (rest - g_mid.astype(F32)).astype(BF16)
    b_all = (jnp.dot(tri, g_hi, preferred_element_type=F32)
             + jnp.dot(tri, g_mid, preferred_element_type=F32)
             + jnp.dot(tri, g_lo, preferred_element_type=F32))

    def prepare(c):
        rows = slice(c * c_len, (c + 1) * c_len)
        g = g_all[rows]
        b = b_all[rows]
        b_last = b[c_len - 1:c_len, :]
        b_mid = b[c_len // 2 - 1:c_len // 2, :]
        q_mag = jnp.max(jnp.abs(rq_ref[rows, :].astype(F32)), axis=0, keepdims=True)
        log2_factor = jnp.maximum(-b_mid, b_mid - b_last) + jnp.log2(jnp.maximum(q_mag, 1.0))
        safe = jnp.max(log2_factor) < SAFE_LOG2
        return rows, g, b, b_last, b_mid, safe

    def factored_scores(q, kk, b, b_mid):
        qb = (q * jnp.exp2(b - b_mid)).astype(BF16)
        kh = (kk * jnp.exp2(b_mid - b)).astype(BF16)

        def head(h):
            sl = slice(h * HEAD_W, (h + 1) * HEAD_W)
            s = lax.dot_general(qb[:, sl], kh[:, sl], _NT, preferred_element_type=F32)
            return jnp.where(causal, s, 0.0)
        return head

    def exact_scores(q, kk, b):
        def head(h):
            sl = slice(h * HEAD_W, (h + 1) * HEAD_W)
            qh, bh, kh = q[:, sl], b[:, sl], kk[:, sl]

            def col_body(s_idx, scores):
                pick = row_col == s_idx
                b_row = jnp.sum(jnp.where(pick, bh, 0.0), axis=0, keepdims=True)
                k_row = jnp.sum(jnp.where(pick, kh, 0.0), axis=0, keepdims=True)
                e = jnp.exp2(jnp.minimum(bh - b_row, 0.0))
                col = jnp.sum(qh * k_row * e, axis=1, keepdims=True)
                col = jnp.where(row_col >= s_idx, col, 0.0)
                return jnp.where(cols_i == s_idx, col, scores)

            return lax.fori_loop(0, c_len, col_body, jnp.zeros((c_len, c_len), F32))
        return head

    def finish(rows, q, kk, b, b_last, scores_of_head, state):
        qb = (q * jnp.exp2(b)).astype(BF16)
        kdb = (kk * jnp.exp2(b_last - b)).astype(BF16)
        decay_last = jnp.exp2(b_last)
        lanes = [slice(h * HEAD_W, (h + 1) * HEAD_W) for h in range(HEADS)]
        vs = [ri_ref[rows, sl] for sl in lanes]
        sc = [scores_of_head(h).astype(BF16) for h in range(HEADS)]
        upd = [lax.dot_general(vs[h], kdb[:, lanes[h]], _TN, preferred_element_type=F32)
               for h in range(HEADS)]
        intra = [jnp.dot(sc[h], vs[h], preferred_element_type=F32) for h in range(HEADS)]
        inter = [lax.dot_general(qb[:, lanes[h]], state[h].astype(BF16), _NT,
                                 preferred_element_type=F32) for h in range(HEADS)]
        for h, sl in enumerate(lanes):
            state[h] = state[h] * decay_last[:, sl] + upd[h]
            o = intra[h] + inter[h]
            o_ref[rows, sl] = (_rms(o) * gate_ref[rows, sl].astype(F32)).astype(o_ref.dtype)

    prepared = [prepare(c) for c in range(n_chunks)]
    all_safe = functools.reduce(jnp.logical_and, [p[5] for p in prepared])

    state = [st_scr[h] for h in range(HEADS)]
    for rows, g, b, b_last, b_mid, _ in prepared:
        q = rq_ref[rows, :].astype(F32)
        kk = 1.0 - jnp.exp2(g)
        finish(rows, q, kk, b, b_last, factored_scores(q, kk, b, b_mid), state)

    @pl.when(all_safe)
    def _():
        for h in range(HEADS):
            st_scr[h] = state[h]

    @pl.when(jnp.logical_not(all_safe))
    def _():
        redo_state = [st_scr[h] for h in range(HEADS)]
        for rows, g, b, b_last, b_mid, safe in prepared:
            q = rq_ref[rows, :].astype(F32)
            kk = 1.0 - jnp.exp2(g)

            @pl.when(safe)
            def _():
                head = factored_scores(q, kk, b, b_mid)
                for h in range(HEADS):
                    sc_scr[h] = head(h)

            @pl.when(jnp.logical_not(safe))
            def _():
                head = exact_scores(q, kk, b)
                for h in range(HEADS):
                    sc_scr[h] = head(h)

            finish(rows, q, kk, b, b_last, lambda h: sc_scr[h], redo_state)
        for h in range(HEADS):
            st_scr[h] = redo_state[h]


def _hgrn(rq, g, ri, gate, *, batch, seq, tt):
    t = rq.shape[0]
    tiles = seq // tt
    row = lambda bi, ti: (bi * tiles + ti, 0)
    kernel = functools.partial(_hgrn_kernel, n_chunks=tt // REC_CHUNK)
    return pl.pallas_call(
        kernel,
        grid=(batch, tiles),
        in_specs=[pl.BlockSpec((tt, D_MODEL), row)] * 4,
        out_specs=pl.BlockSpec((tt, D_MODEL), row),
        out_shape=jax.ShapeDtypeStruct((t, D_MODEL), BF16),
        scratch_shapes=[pltpu.VMEM((HEADS, HEAD_W, HEAD_W), F32),
                        pltpu.VMEM((HEADS, REC_CHUNK, REC_CHUNK), F32)],
        compiler_params=pltpu.CompilerParams(
            dimension_semantics=("arbitrary", "arbitrary"), vmem_limit_bytes=VMEM_LIMIT),
        name="hgrn",
    )(rq, g, ri, gate)


def _merge_kernel(oa_ref, or_ref, ga_ref, gr_ref, x_ref, ada_ref, wa_ref, wr_ref, wo_ref, o_ref):
    ya = lax.dot_general(oa_ref[0], wa_ref[...], _TN, preferred_element_type=F32)
    yr = jnp.dot(or_ref[...], wr_ref[...], preferred_element_type=F32)
    y = ga_ref[...].astype(F32) * ya + gr_ref[...].astype(F32) * yr
    upd = jnp.dot(y.astype(BF16), wo_ref[...], preferred_element_type=F32)
    o_ref[...] = x_ref[...] + ada_ref[0][2:3] * upd


def _merge(oa, orr, ga, gr, x2d, ada, wa, wr, wo, *, seq, tm):
    t = x2d.shape[0]
    tiles_per_batch = seq // tm
    row = lambda i: (i, 0)
    wspec = pl.BlockSpec((D_MODEL, D_MODEL), lambda i: (0, 0))
    return pl.pallas_call(
        _merge_kernel,
        grid=(t // tm,),
        in_specs=[pl.BlockSpec((1, D_MODEL, tm),
                               lambda i: (i // tiles_per_batch, 0, i % tiles_per_batch))]
                 + [pl.BlockSpec((tm, D_MODEL), row)] * 4
                 + [pl.BlockSpec((1, 8, D_MODEL), lambda i: (i // tiles_per_batch, 0, 0)),
                    wspec, wspec, wspec],
        out_specs=pl.BlockSpec((tm, D_MODEL), row),
        out_shape=jax.ShapeDtypeStruct((t, D_MODEL), F32),
        compiler_params=pltpu.CompilerParams(
            dimension_semantics=("arbitrary",), vmem_limit_bytes=VMEM_LIMIT),
        name="merge",
    )(oa, orr, ga, gr, x2d, ada, wa, wr, wo)


def _mlp_kernel(x_ref, ada_ref, nw_ref, nf_ref, w1_ref, w2_ref, o_ref):
    x = x_ref[...]
    ada = ada_ref[0]
    h = (_rms(x) * nw_ref[...] * (1.0 + ada[4:5]) + ada[3:4]).astype(BF16)
    acc = jnp.zeros_like(x)
    for c in range(D_FF // D_MODEL):
        cols = slice(c * D_MODEL, (c + 1) * D_MODEL)
        u = jnp.dot(h, w1_ref[:, cols], preferred_element_type=F32)
        u = jnp.square(jnp.maximum(u, 0.0))
        acc = acc + jnp.dot(u.astype(BF16), w2_ref[cols, :], preferred_element_type=F32)
    o_ref[...] = _rms(x + ada[5:6] * acc) * nf_ref[...]


def _mlp(x1, ada, norm_w, norm_final, w1, w2, *, seq, tm):
    t = x1.shape[0]
    tiles_per_batch = seq // tm
    row = lambda i: (i, 0)
    const = lambda i: (0, 0)
    return pl.pallas_call(
        _mlp_kernel,
        grid=(t // tm,),
        in_specs=[pl.BlockSpec((tm, D_MODEL), row),
                  pl.BlockSpec((1, 8, D_MODEL), lambda i: (i // tiles_per_batch, 0, 0)),
                  pl.BlockSpec((1, D_MODEL), const),
                  pl.BlockSpec((1, D_MODEL), const),
                  pl.BlockSpec((D_MODEL, D_FF), const, pipeline_mode=pl.Buffered(1)),
                  pl.BlockSpec((D_FF, D_MODEL), const, pipeline_mode=pl.Buffered(1))],
        out_specs=pl.BlockSpec((tm, D_MODEL), row),
        out_shape=jax.ShapeDtypeStruct((t, D_MODEL), F32),
        compiler_params=pltpu.CompilerParams(
            dimension_semantics=("arbitrary",), vmem_limit_bytes=VMEM_LIMIT),
        name="mlp",
    )(x1, ada, norm_w, norm_final, w1, w2)


def _rope_inv_freq_lanes():
    inv_freq = ROPE_THETA ** (-jnp.arange(0, ROPE_DIM, 2, dtype=F32) / ROPE_DIM)
    lane = jnp.arange(HEAD_W) % QK_DIM
    table = jnp.where(lane < ROPE_DIM, inv_freq[lane % (ROPE_DIM // 2)], 0.0)
    return table.reshape(1, HEAD_W).astype(F32)


def kernel(x, c, positions, w_ada, b_ada, norm_mix, w_in, lam_q1, lam_k1, lam_q2, lam_k2, subln_w, lb_logits, rec_norm_w, w_proj_att, w_proj_rec, w_out, norm_mlp, w_mlp_in, w_mlp_out, norm_final):
    batch, seq, d = x.shape
    assert d == D_MODEL and w_ada.shape[0] == 1, "single-layer, d_model=1024 only"
    assert batch <= 8
    t = batch * seq
    tm = min(512, seq)
    tq = min(512, seq)
    tt = min(256, seq)
    assert seq % tm == 0 and seq % tq == 0 and seq % tt == 0 and tt % REC_CHUNK == 0

    x2d = x.reshape(t, D_MODEL)
    c_pad = jnp.zeros((8, D_MODEL), F32).at[:batch].set(c)
    ada = _ada(c_pad, w_ada[0], b_ada[0].reshape(1, -1))[:batch]
    ada = jnp.pad(ada.reshape(batch, N_ADA, D_MODEL), ((0, 0), (0, 8 - N_ADA), (0, 0)))

    q, k, v, rq, g, ri, gate, ga, gr = _inproj(
        x2d, ada, norm_mix[0].reshape(1, -1), w_in[0].astype(BF16),
        positions.reshape(t, 1), _rope_inv_freq_lanes(), lb_logits,
        jnp.tile(rec_norm_w.reshape(1, HEAD_W), (1, HEADS)), seq=seq, tm=tm)

    as3 = lambda a: a.reshape(batch, seq, D_MODEL)
    o_a = _attn(as3(q), as3(k), as3(v), lam_q1, lam_k1, lam_q2, lam_k2,
                subln_w.reshape(HEAD_W, 1), tq=tq)
    o_r = _hgrn(rq, g, ri, gate, batch=batch, seq=seq, tt=tt)

    x1 = _merge(o_a, o_r, ga, gr, x2d, ada,
                w_proj_att[0].astype(BF16), w_proj_rec[0].astype(BF16), w_out[0].astype(BF16),
                seq=seq, tm=tm)
    out = _mlp(x1, ada, norm_mlp[0].reshape(1, -1), norm_final.reshape(1, -1),
               w_mlp_in[0].astype(BF16), w_mlp_out[0].astype(BF16), seq=seq, tm=tm)
    return out.reshape(batch, seq, D_MODEL)
```

```python
import functools
import math

import jax
import jax.numpy as jnp
from jax import lax
from jax.experimental import pallas as pl
from jax.experimental.pallas import tpu as pltpu

F32 = jnp.float32
BF16 = jnp.bfloat16

D_MODEL = 1024
HEADS = 8
HEAD_W = 128
QK_DIM = 64
ROPE_DIM = 16
ROPE_THETA = 500000.0
D_FF = 4 * D_MODEL
N_ADA = 6
N_SECTIONS = 9
REC_CHUNK = 64
EPS = 1e-6
LAM_INIT = 0.8 - 0.6 * math.exp(-0.3 * 0)
LOG2_E = 1.0 / math.log(2.0)
SAFE_LOG2 = 115.0
MAX_RISE = 64.0
SUB_KEYS = 256
STREAM_TILES = 4
QK_AHEAD = 2
HEADS_PER_STEP = 4
ROW_GROUPS = 2
VMEM_LIMIT = 56 * 1024 * 1024

_NT = (((1,), (1,)), ((), ()))
_TN = (((0,), (0,)), ((), ()))


def _rms(x):
    return x * lax.rsqrt(jnp.mean(x * x, axis=-1, keepdims=True) + EPS)


def _ada_kernel(c_ref, w_ref, b_ref, o_ref):
    c = c_ref[...]
    cond = c * jax.nn.sigmoid(c)
    o_ref[...] = jnp.dot(cond, w_ref[...], preferred_element_type=F32,
                         precision=lax.Precision.HIGHEST) + b_ref[...]


def _ada(c_pad, w_ada, b_ada):
    n = w_ada.shape[1]
    return pl.pallas_call(
        _ada_kernel,
        grid=(n // D_MODEL,),
        in_specs=[pl.BlockSpec((8, D_MODEL), lambda j: (0, 0)),
                  pl.BlockSpec((D_MODEL, D_MODEL), lambda j: (0, j)),
                  pl.BlockSpec((1, D_MODEL), lambda j: (0, j))],
        out_specs=pl.BlockSpec((8, D_MODEL), lambda j: (0, j)),
        out_shape=jax.ShapeDtypeStruct((8, n), F32),
        name="ada",
    )(c_pad, w_ada, b_ada)


def _inproj_kernel(x_ref, ada_ref, nw_ref, w_ref, pos_ref, invf_ref, lb_ref, rnw_ref,
                   q_ref, k_ref, v_ref, rq_ref, g_ref, ri_ref, gate_ref, ga_ref, gr_ref):
    ada = ada_ref[0]
    rows = x_ref.shape[0] // ROW_GROUPS
    hs = [(_rms(x_ref[r * rows:(r + 1) * rows, :]) * nw_ref[...] * (1.0 + ada[1:2]) + ada[0:1]).astype(BF16)
          for r in range(ROW_GROUPS)]

    ang = pos_ref[...].astype(F32) * invf_ref[...]
    lane = lax.broadcasted_iota(jnp.int32, (1, HEAD_W), 1) % QK_DIM
    cos_a = jnp.cos(ang)
    sin_a = jnp.sin(ang)
    cs = jnp.where(lane < ROPE_DIM, cos_a, 1.0)
    s1 = jnp.where(lane < ROPE_DIM // 2, -sin_a, 0.0)
    s2 = jnp.where((lane >= ROPE_DIM // 2) & (lane < ROPE_DIM), sin_a, 0.0)

    def section(j):
        w = w_ref[:, j * D_MODEL:(j + 1) * D_MODEL]
        return jnp.concatenate([jnp.dot(hh, w, preferred_element_type=F32) for hh in hs], axis=0)

    def rope_store(ref, acc, scale):
        c_, s1_, s2_ = cs * scale, s1 * scale, s2 * scale
        for hh in range(HEADS):
            t = acc[:, hh * HEAD_W:(hh + 1) * HEAD_W]
            r = (t * c_ + pltpu.roll(t, HEAD_W - ROPE_DIM // 2, 1) * s1_
                 + pltpu.roll(t, ROPE_DIM // 2, 1) * s2_)
            ref[:, hh * HEAD_W:(hh + 1) * HEAD_W] = r.astype(ref.dtype)

    v_ref[...] = section(2).astype(v_ref.dtype)
    rope_store(q_ref, section(0), QK_DIM ** -0.5 * LOG2_E)
    rq_ref[...] = section(3).astype(rq_ref.dtype)
    rope_store(k_ref, section(1), 1.0)
    ri_ref[...] = section(5).astype(ri_ref.dtype)

    l0 = lb_ref[0:1, :]
    l1 = lb_ref[1:2, :]
    mx = jnp.maximum(l0, l1)
    e0 = jnp.exp(l0 - mx)
    e1 = jnp.exp(l1 - mx)
    lb = e1 / (e0 + e1)
    g_ref[...] = jnp.log(lb + (1.0 - lb) * jax.nn.sigmoid(section(4))) * LOG2_E

    rg = section(6)
    gate_ref[...] = (rnw_ref[...] * (rg * jax.nn.sigmoid(rg))).astype(gate_ref.dtype)
    ga_ref[...] = jax.nn.sigmoid(section(7)).astype(ga_ref.dtype)
    gr_ref[...] = jax.nn.sigmoid(section(8)).astype(gr_ref.dtype)


def _inproj(x2d, ada, norm_w, w_in, pos, invf, lb_logits, rec_norm_row, *, seq, tm):
    t = x2d.shape[0]
    tiles_per_batch = seq // tm
    row = lambda i: (i, 0)
    const2 = lambda i: (0, 0)
    out_dtypes = (BF16, BF16, BF16, BF16, F32, BF16, BF16, BF16, BF16)
    return pl.pallas_call(
        _inproj_kernel,
        grid=(t // tm,),
        in_specs=[pl.BlockSpec((tm, D_MODEL), row),
                  pl.BlockSpec((1, 8, D_MODEL), lambda i: (i // tiles_per_batch, 0, 0)),
                  pl.BlockSpec((1, D_MODEL), const2),
                  pl.BlockSpec((D_MODEL, N_SECTIONS * D_MODEL), const2, pipeline_mode=pl.Buffered(1)),
                  pl.BlockSpec((tm, 1), row),
                  pl.BlockSpec((1, HEAD_W), const2),
                  pl.BlockSpec((2, D_MODEL), const2),
                  pl.BlockSpec((1, D_MODEL), const2)],
        out_specs=[pl.BlockSpec((tm, D_MODEL), row) for _ in out_dtypes],
        out_shape=[jax.ShapeDtypeStruct((t, D_MODEL), dt) for dt in out_dtypes],
        compiler_params=pltpu.CompilerParams(
            dimension_semantics=("arbitrary",), vmem_limit_bytes=VMEM_LIMIT),
        name="inproj",
    )(x2d, ada, norm_w, w_in, pos, invf, lb_logits, rec_norm_row)


def _attn_kernel(lq1_ref, lk1_ref, lq2_ref, lk2_ref, sw_ref, q_ref, k_ref, v_ref, o_ref,
                 vt_scr, acc, m, l, dev, *, tq):
    qi = pl.program_id(2)
    n_tiles = pl.num_programs(2)
    n_streams = 2 * HEADS_PER_STEP

    @pl.when(qi == 0)
    def _():
        def tr(j, carry):
            start = pl.multiple_of(j * tq, tq)
            vt_scr[:, pl.ds(start, tq)] = v_ref[0, pl.ds(start, tq), :].T
            return carry
        lax.fori_loop(0, n_tiles, tr, 0)

    def head_lanes(s):
        return slice((s // 2) * HEAD_W, (s // 2 + 1) * HEAD_W)

    q = q_ref[0]
    lane = lax.broadcasted_iota(jnp.int32, (1, HEAD_W), 1)
    q_streams = []
    for s in range(n_streams):
        qh = q[:, head_lanes(s)]
        keep = (lane < QK_DIM) if s % 2 == 0 else (lane >= QK_DIM)
        q_streams.append(jnp.where(keep, qh, jnp.zeros_like(qh)))

    def init():
        acc[...] = jnp.zeros_like(acc)
        m[...] = jnp.full_like(m, -jnp.inf)
        l[...] = jnp.zeros_like(l)
        dev[...] = jnp.zeros_like(dev)

    def tiles(kj):
        start = pl.multiple_of(kj * tq, tq)
        return k_ref[0, pl.ds(start, tq), :], vt_scr[:, pl.ds(start, tq)]

    def exact_step(kj):
        k, vt = tiles(kj)
        for s in range(n_streams):
            hl = head_lanes(s)
            st = lax.dot_general(k[:, hl], q_streams[s], _NT, preferred_element_type=F32)
            kr = lax.broadcasted_iota(jnp.int32, st.shape, 0) + (kj - qi) * tq
            qc = lax.broadcasted_iota(jnp.int32, st.shape, 1)
            st = jnp.where(kr <= qc, st, -jnp.inf)
            m_prev = m[s]
            m_new = jnp.maximum(m_prev, jnp.max(st, axis=0, keepdims=True))
            alpha = jnp.exp2(m_prev - m_new)
            pt = jnp.exp2(st - m_new)
            l[s] = alpha * l[s] + jnp.sum(pt, axis=0, keepdims=True)
            acc[s] = alpha * acc[s] + jnp.dot(vt[hl, :], pt.astype(BF16), preferred_element_type=F32)
            m[s] = m_new

    def stream_tiles(kj0, n, after_first_scores=None):
        kv = [tiles(kj0 + t) for t in range(n)]
        n_sub = tq // SUB_KEYS
        chains = [(t, s, h) for t in range(n) for s in range(n_streams) for h in range(n_sub)]

        def qk(c):
            t, s, h = c
            ks = slice(h * SUB_KEYS, (h + 1) * SUB_KEYS)
            return lax.dot_general(kv[t][0][ks, head_lanes(s)], q_streams[s], _NT,
                                   preferred_element_type=F32)

        scores = {i: qk(c) for i, c in enumerate(chains[:QK_AHEAD])}
        if after_first_scores is not None:
            after_first_scores()
        part = {}
        for i, (t, s, h) in enumerate(chains):
            st = scores.pop(i)
            m_prev = m[s]
            pt = jnp.exp2(st - m_prev)
            mx = jnp.max(st, axis=0, keepdims=True)
            sm = jnp.sum(pt, axis=0, keepdims=True)
            if i + QK_AHEAD < len(chains):
                scores[i + QK_AHEAD] = qk(chains[i + QK_AHEAD])
            ks = slice(h * SUB_KEYS, (h + 1) * SUB_KEYS)
            pp = jnp.dot(kv[t][1][head_lanes(s), ks], pt.astype(BF16), preferred_element_type=F32)
            if h == 0:
                part[s] = (mx, sm, pp)
            else:
                mx0, sm0, pp0 = part[s]
                part[s] = (jnp.maximum(mx0, mx), sm0 + sm, pp0 + pp)
            if h == n_sub - 1:
                t_max, l_add, pv = part.pop(s)
                m_new = jnp.maximum(m_prev, t_max)
                alpha = jnp.exp2(m_prev - m_new)
                dev[s] = jnp.maximum(dev[s], t_max - m_prev)
                l[s] = alpha * (l[s] + l_add)
                acc[s] = alpha * (acc[s] + pv)
                m[s] = m_new

    def diag_step(then_stream):
        k, vt = tiles(qi)
        half = tq // 2
        problems = ((slice(0, half), slice(0, half)), (slice(0, tq), slice(half, tq)))
        sts = {(s, pi): lax.dot_general(k[ks, head_lanes(s)], q_streams[s][qs], _NT,
                                        preferred_element_type=F32)
               for s in range(n_streams) for pi, (ks, qs) in enumerate(problems)}
        dev[...] = jnp.zeros_like(dev)
        pending = []
        for s in range(n_streams):
            for pi, (ks, qs) in enumerate(problems):
                st = sts[s, pi]
                kr = lax.broadcasted_iota(jnp.int32, st.shape, 0)
                qc = lax.broadcasted_iota(jnp.int32, st.shape, 1) + qs.start
                st = jnp.where(kr <= qc, st, -jnp.inf)
                m_new = jnp.max(st, axis=0, keepdims=True)
                pt = jnp.exp2(st - m_new)
                l[s, :, qs] = jnp.sum(pt, axis=0, keepdims=True)
                m[s, :, qs] = m_new
                pending.append((s, ks, qs, pt.astype(BF16)))

        def value_matmuls():
            for s, ks, qs, ptb in pending:
                acc[s, :, qs] = jnp.dot(vt[head_lanes(s), ks], ptb, preferred_element_type=F32)

        if then_stream:
            stream_tiles(0, then_stream, after_first_scores=value_matmuls)
        else:
            value_matmuls()

    first = qi % STREAM_TILES
    for r in range(STREAM_TILES):
        @pl.when(first == r)
        def _(r=r):
            diag_step(r)

    def body(t, carry):
        stream_tiles(first + STREAM_TILES * t, STREAM_TILES)
        return carry

    lax.fori_loop(0, qi // STREAM_TILES, body, 0)

    lam = (jnp.exp(jnp.sum(lq1_ref[...] * lk1_ref[...], axis=-1, keepdims=True))
           - jnp.exp(jnp.sum(lq2_ref[...] * lk2_ref[...], axis=-1, keepdims=True)) + LAM_INIT)

    def write_output():
        for hh in range(HEADS_PER_STEP):
            s1, s2 = 2 * hh, 2 * hh + 1
            ot = acc[s1] / l[s1] - lam * (acc[s2] / l[s2])
            inv = lax.rsqrt(jnp.mean(ot * ot, axis=0, keepdims=True) + EPS)
            ot = ot * inv * sw_ref[...] * (1.0 - LAM_INIT)
            o_ref[0, hh * HEAD_W:(hh + 1) * HEAD_W, :] = ot.astype(o_ref.dtype)

    write_output()

    rise = jnp.max(dev[...])

    @pl.when(jnp.logical_not(rise <= MAX_RISE))
    def _():
        init()

        def redo(kj, carry):
            exact_step(kj)
            return carry
        lax.fori_loop(0, qi + 1, redo, 0)
        write_output()


def _attn(q3, k3, v3, lq1, lk1, lq2, lk2, subln_col, *, tq):
    b, s, _ = q3.shape
    vec = lambda bi, h, qi: (0, 0)
    width = HEADS_PER_STEP * HEAD_W
    n_streams = 2 * HEADS_PER_STEP
    kernel = functools.partial(_attn_kernel, tq=tq)
    return pl.pallas_call(
        kernel,
        grid=(b, HEADS // HEADS_PER_STEP, s // tq),
        in_specs=[pl.BlockSpec((1, QK_DIM), vec)] * 4
                 + [pl.BlockSpec((HEAD_W, 1), vec),
                    pl.BlockSpec((1, tq, width), lambda bi, h, qi: (bi, qi, h)),
                    pl.BlockSpec((1, s, width), lambda bi, h, qi: (bi, 0, h)),
                    pl.BlockSpec((1, s, width), lambda bi, h, qi: (bi, 0, h))],
        out_specs=pl.BlockSpec((1, width, tq), lambda bi, h, qi: (bi, h, qi)),
        out_shape=jax.ShapeDtypeStruct((b, D_MODEL, s), BF16),
        scratch_shapes=[pltpu.VMEM((width, s), BF16),
                        pltpu.VMEM((n_streams, HEAD_W, tq), F32),
                        pltpu.VMEM((n_streams, 1, tq), F32),
                        pltpu.VMEM((n_streams, 1, tq), F32),
                        pltpu.VMEM((n_streams, 1, tq), F32)],
        compiler_params=pltpu.CompilerParams(
            dimension_semantics=("arbitrary", "arbitrary", "arbitrary"), vmem_limit_bytes=VMEM_LIMIT),
        name="attn",
    )(lq1, lk1, lq2, lk2, subln_col, q3, k3, v3)


def _hgrn_kernel(rq_ref, g_ref, ri_ref, gate_ref, o_ref, st_scr, sc_scr, *, n_chunks):
    c_len = REC_CHUNK

    @pl.when(pl.program_id(1) == 0)
    def _():
        st_scr[...] = jnp.zeros_like(st_scr)

    rows_i = lax.broadcasted_iota(jnp.int32, (c_len, c_len), 0)
    cols_i = lax.broadcasted_iota(jnp.int32, (c_len, c_len), 1)
    causal = rows_i >= cols_i
    row_col = lax.broadcasted_iota(jnp.int32, (c_len, 1), 0)

    n_rows = n_chunks * c_len
    t_row = lax.broadcasted_iota(jnp.int32, (n_rows, n_rows), 0)
    t_col = lax.broadcasted_iota(jnp.int32, (n_rows, n_rows), 1)
    tri = ((t_row // c_len == t_col // c_len) & (t_row >= t_col)).astype(BF16)
    g_all = g_ref[...]
    g_hi = g_all.astype(BF16)
    rest = g_all - g_hi.astype(F32)
    g_mid = rest.astype(BF16)
    g_lo = (rest - g_mid.astype(F32)).astype(BF16)
    b_all = (jnp.dot(tri, g_hi, preferred_element_type=F32)
             + jnp.dot(tri, g_mid, preferred_element_type=F32)
             + jnp.dot(tri, g_lo, preferred_element_type=F32))

    def prepare(c):
        rows = slice(c * c_len, (c + 1) * c_len)
        g = g_all[rows]
        b = b_all[rows]
        b_last = b[c_len - 1:c_len, :]
        b_mid = b[c_len // 2 - 1:c_len // 2, :]
        q_mag = jnp.max(jnp.abs(rq_ref[rows, :].astype(F32)), axis=0, keepdims=True)
        log2_factor = jnp.maximum(-b_mid, b_mid - b_last) + jnp.log2(jnp.maximum(q_mag, 1.0))
        safe = jnp.max(log2_factor) < SAFE_LOG2
        return rows, g, b, b_last, b_mid, safe

    def factored_scores(q, kk, b, b_mid):
        qb = (q * jnp.exp2(b - b_mid)).astype(BF16)
        kh = (kk * jnp.exp2(b_mid - b)).astype(BF16)

        def head(h):
            sl = slice(h * HEAD_W, (h + 1) * HEAD_W)
            s = lax.dot_general(qb[:, sl], kh[:, sl], _NT, preferred_element_type=F32)
            return jnp.where(causal, s, 0.0)
        return head

    def exact_scores(q, kk, b):
        def head(h):
            sl = slice(h * HEAD_W, (h + 1) * HEAD_W)
            qh, bh, kh = q[:, sl], b[:, sl], kk[:, sl]

            def col_body(s_idx, scores):
                pick = row_col == s_idx
                b_row = jnp.sum(jnp.where(pick, bh, 0.0), axis=0, keepdims=True)
                k_row = jnp.sum(jnp.where(pick, kh, 0.0), axis=0, keepdims=True)
                e = jnp.exp2(jnp.minimum(bh - b_row, 0.0))
                col = jnp.sum(qh * k_row * e, axis=1, keepdims=True)
                col = jnp.where(row_col >= s_idx, col, 0.0)
                return jnp.where(cols_i == s_idx, col, scores)

            return lax.fori_loop(0, c_len, col_body, jnp.zeros((c_len, c_len), F32))
        return head

    def finish(rows, q, kk, b, b_last, scores_of_head, state):
        qb = (q * jnp.exp2(b)).astype(BF16)
        kdb = (kk * jnp.exp2(b_last - b)).astype(BF16)
        decay_last = jnp.exp2(b_last)
        lanes = [slice(h * HEAD_W, (h + 1) * HEAD_W) for h in range(HEADS)]
        vs = [ri_ref[rows, sl] for sl in lanes]
        sc = [scores_of_head(h).astype(BF16) for h in range(HEADS)]
        upd = [lax.dot_general(vs[h], kdb[:, lanes[h]], _TN, preferred_element_type=F32)
               for h in range(HEADS)]
        intra = [jnp.dot(sc[h], vs[h], preferred_element_type=F32) for h in range(HEADS)]
        inter = [lax.dot_general(qb[:, lanes[h]], state[h].astype(BF16), _NT,
                                 preferred_element_type=F32) for h in range(HEADS)]
        for h, sl in enumerate(lanes):
            state[h] = state[h] * decay_last[:, sl] + upd[h]
            o = intra[h] + inter[h]
            o_ref[rows, sl] = (_rms(o) * gate_ref[rows, sl].astype(F32)).astype(o_ref.dtype)

    prepared = [prepare(c) for c in range(n_chunks)]
    all_safe = functools.reduce(jnp.logical_and, [p[5] for p in prepared])

    state = [st_scr[h] for h in range(HEADS)]
    for rows, g, b, b_last, b_mid, _ in prepared:
        q = rq_ref[rows, :].astype(F32)
        kk = 1.0 - jnp.exp2(g)
        finish(rows, q, kk, b, b_last, factored_scores(q, kk, b, b_mid), state)

    @pl.when(all_safe)
    def _():
        for h in range(HEADS):
            st_scr[h] = state[h]

    @pl.when(jnp.logical_not(all_safe))
    def _():
        redo_state = [st_scr[h] for h in range(HEADS)]
        for rows, g, b, b_last, b_mid, safe in prepared:
            q = rq_ref[rows, :].astype(F32)
            kk = 1.0 - jnp.exp2(g)

            @pl.when(safe)
            def _():
                head = factored_scores(q, kk, b, b_mid)
                for h in range(HEADS):
                    sc_scr[h] = head(h)

            @pl.when(jnp.logical_not(safe))
            def _():
                head = exact_scores(q, kk, b)
                for h in range(HEADS):
                    sc_scr[h] = head(h)

            finish(rows, q, kk, b, b_last, lambda h: sc_scr[h], redo_state)
        for h in range(HEADS):
            st_scr[h] = redo_state[h]


def _hgrn(rq, g, ri, gate, *, batch, seq, tt):
    t = rq.shape[0]
    tiles = seq // tt
    row = lambda bi, ti: (bi * tiles + ti, 0)
    kernel = functools.partial(_hgrn_kernel, n_chunks=tt // REC_CHUNK)
    return pl.pallas_call(
        kernel,
        grid=(batch, tiles),
        in_specs=[pl.BlockSpec((tt, D_MODEL), row)] * 4,
        out_specs=pl.BlockSpec((tt, D_MODEL), row),
        out_shape=jax.ShapeDtypeStruct((t, D_MODEL), BF16),
        scratch_shapes=[pltpu.VMEM((HEADS, HEAD_W, HEAD_W), F32),
                        pltpu.VMEM((HEADS, REC_CHUNK, REC_CHUNK), F32)],
        compiler_params=pltpu.CompilerParams(
            dimension_semantics=("arbitrary", "arbitrary"), vmem_limit_bytes=VMEM_LIMIT),
        name="hgrn",
    )(rq, g, ri, gate)


def _tail_kernel(oa_ref, or_ref, ga_ref, gr_ref, x_ref, ada_ref, nw_ref, nf_ref,
                 wa_ref, wr_ref, wo_ref, w1_ref, w2_ref, o_ref):
    ada = ada_ref[0]
    ya = lax.dot_general(oa_ref[0], wa_ref[...], _TN, preferred_element_type=F32)
    yr = jnp.dot(or_ref[...], wr_ref[...], preferred_element_type=F32)
    y = ga_ref[...].astype(F32) * ya + gr_ref[...].astype(F32) * yr
    upd = jnp.dot(y.astype(BF16), wo_ref[...], preferred_element_type=F32)
    x = x_ref[...] + ada[2:3] * upd
    h = (_rms(x) * nw_ref[...] * (1.0 + ada[4:5]) + ada[3:4]).astype(BF16)
    acc = jnp.zeros_like(x)
    for c in range(D_FF // D_MODEL):
        cols = slice(c * D_MODEL, (c + 1) * D_MODEL)
        u = jnp.dot(h, w1_ref[:, cols], preferred_element_type=F32)
        u = jnp.square(jnp.maximum(u, 0.0))
        acc = acc + jnp.dot(u.astype(BF16), w2_ref[cols, :], preferred_element_type=F32)
    o_ref[...] = _rms(x + ada[5:6] * acc) * nf_ref[...]


def _tail(oa, orr, ga, gr, x2d, ada, norm_w, norm_final, wa, wr, wo, w1, w2, *, seq, tm):
    t = x2d.shape[0]
    tiles_per_batch = seq // tm
    row = lambda i: (i, 0)
    const = lambda i: (0, 0)
    single = pl.Buffered(1)
    wspec = pl.BlockSpec((D_MODEL, D_MODEL), const, pipeline_mode=single)
    return pl.pallas_call(
        _tail_kernel,
        grid=(t // tm,),
        in_specs=[pl.BlockSpec((1, D_MODEL, tm),
                               lambda i: (i // tiles_per_batch, 0, i % tiles_per_batch))]
                 + [pl.BlockSpec((tm, D_MODEL), row)] * 4
                 + [pl.BlockSpec((1, 8, D_MODEL), lambda i: (i // tiles_per_batch, 0, 0)),
                    pl.BlockSpec((1, D_MODEL), const),
                    pl.BlockSpec((1, D_MODEL), const),
                    wspec, wspec, wspec,
                    pl.BlockSpec((D_MODEL, D_FF), const, pipeline_mode=single),
                    pl.BlockSpec((D_FF, D_MODEL), const, pipeline_mode=single)],
        out_specs=pl.BlockSpec((tm, D_MODEL), row),
        out_shape=jax.ShapeDtypeStruct((t, D_MODEL), F32),
        compiler_params=pltpu.CompilerParams(
            dimension_semantics=("arbitrary",), vmem_limit_bytes=VMEM_LIMIT),
        name="tail",
    )(oa, orr, ga, gr, x2d, ada, norm_w, norm_final, wa, wr, wo, w1, w2)


def _rope_inv_freq_lanes():
    inv_freq = ROPE_THETA ** (-jnp.arange(0, ROPE_DIM, 2, dtype=F32) / ROPE_DIM)
    lane = jnp.arange(HEAD_W) % QK_DIM
    table = jnp.where(lane < ROPE_DIM, inv_freq[lane % (ROPE_DIM // 2)], 0.0)
    return table.reshape(1, HEAD_W).astype(F32)


def kernel(x, c, positions, w_ada, b_ada, norm_mix, w_in, lam_q1, lam_k1, lam_q2, lam_k2, subln_w, lb_logits, rec_norm_w, w_proj_att, w_proj_rec, w_out, norm_mlp, w_mlp_in, w_mlp_out, norm_final):
    batch, seq, d = x.shape
    assert d == D_MODEL and w_ada.shape[0] == 1, "single-layer, d_model=1024 only"
    assert batch <= 8
    t = batch * seq
    tm = min(512, seq)
    tq = min(512, seq)
    tt = min(256, seq)
    assert seq % tm == 0 and seq % tq == 0 and seq % tt == 0 and tt % REC_CHUNK == 0

    x2d = x.reshape(t, D_MODEL)
    c_pad = jnp.zeros((8, D_MODEL), F32).at[:batch].set(c)
    ada = _ada(c_pad, w_ada[0], b_ada[0].reshape(1, -1))[:batch]
    ada = jnp.pad(ada.reshape(batch, N_ADA, D_MODEL), ((0, 0), (0, 8 - N_ADA), (0, 0)))

    q, k, v, rq, g, ri, gate, ga, gr = _inproj(
        x2d, ada, norm_mix[0].reshape(1, -1), w_in[0].astype(BF16),
        positions.reshape(t, 1), _rope_inv_freq_lanes(), lb_logits,
        jnp.tile(rec_norm_w.reshape(1, HEAD_W), (1, HEADS)), seq=seq, tm=tm)

    as3 = lambda a: a.reshape(batch, seq, D_MODEL)
    o_a = _attn(as3(q), as3(k), as3(v), lam_q1, lam_k1, lam_q2, lam_k2,
                subln_w.reshape(HEAD_W, 1), tq=tq)
    o_r = _hgrn(rq, g, ri, gate, batch=batch, seq=seq, tt=tt)

    out = _tail(o_a, o_r, ga, gr, x2d, ada, norm_mlp[0].reshape(1, -1), norm_final.reshape(1, -1),
                w_proj_att[0].astype(BF16), w_proj_rec[0].astype(BF16), w_out[0].astype(BF16),
                w_mlp_in[0].astype(BF16), w_mlp_out[0].astype(BF16), seq=seq, tm=tm)
    return out.reshape(batch, seq, D_MODEL)
```
